```python
import math
import jax, jax.numpy as jnp
from jax import lax
import numpy as np

D_MODEL = 1024
BATCH = 8
SEQ = 4096
DEPTH = 4

CTX_LEN = 256
GRID_W = 64
N_MIXERS = 2
N_ATTN_LAYERS = (DEPTH + 1) // 2
N_SSD_LAYERS = DEPTH // 2
EPS = 1e-6

ATTN_HEAD_DIM = 64
ATTN_HEADS = D_MODEL // ATTN_HEAD_DIM
ATTN_KV_HEADS = 4
ATTN_GROUP = ATTN_HEADS // ATTN_KV_HEADS
ATTN_WIDTH = ATTN_HEADS * ATTN_HEAD_DIM
ATTN_KV_WIDTH = ATTN_KV_HEADS * ATTN_HEAD_DIM
ATTN_IN = 2 * ATTN_WIDTH + 2 * ATTN_KV_WIDTH
WINDOW = 128
BLOCK = 128
ROPE_BASE = 10000.0

SSD_WIDTH = 2 * D_MODEL
SSD_HEAD_DIM = 64
SSD_HEADS = SSD_WIDTH // SSD_HEAD_DIM
SSD_GROUPS = 8
SSD_HEADS_PER_GROUP = SSD_HEADS // SSD_GROUPS
SSD_STATE = 128
SSD_CONV = 5
SSD_CHUNK = 128
SSD_CONV_DIM = SSD_WIDTH + 2 * SSD_GROUPS * SSD_STATE
SSD_IN = SSD_WIDTH + SSD_CONV_DIM + 2 * SSD_HEADS

kernel_name = "hybrid_swa_ssd_diffusion_prefix"


def rms_normalize(x):
    xf = x.astype(jnp.float32)
    return (xf * lax.rsqrt(jnp.mean(xf * xf, axis=-1, keepdims=True) + EPS)).astype(x.dtype)


def adaln(cond, w, b):
    mod = jax.nn.silu(cond) @ w + b
    return jnp.split(mod, 3, axis=-1)


def rope_1d(x, pos):
    half = x.shape[-1]
    inv_freq = ROPE_BASE ** (-jnp.arange(0, half, 2, dtype=jnp.float32) / half)
    ang = pos.astype(jnp.float32)[:, None] * inv_freq[None, :]
    ang = jnp.concatenate([ang, ang], axis=-1)
    shape = (pos.shape[0],) + (1,) * (x.ndim - 3) + (half,)
    cos = jnp.cos(ang).reshape(shape)
    sin = jnp.sin(ang).reshape(shape)
    x1, x2 = jnp.split(x, 2, axis=-1)
    rot = jnp.concatenate([-x2, x1], axis=-1)
    return (x * cos + rot * sin).astype(x.dtype)


def axial_rope(x, row, col):
    half = x.shape[-1] // 2
    return jnp.concatenate([rope_1d(x[..., :half], row), rope_1d(x[..., half:], col)], axis=-1)


def sink_softmax(scores, sink_kg, mask):
    if mask is not None:
        scores = jnp.where(mask, scores, -jnp.inf)
    sink_col = jnp.broadcast_to(sink_kg.astype(jnp.float32)[None, :, :, None, None],
                                scores.shape[:-1] + (1,))
    p = jax.nn.softmax(jnp.concatenate([scores, sink_col], axis=-1), axis=-1)
    return p[..., :-1]


def attention_mixer(h_lat, h_ctx, w_in, sink, w_out, row, col, need_ctx_out):
    bsz, L, _ = h_lat.shape

    def project(h):
        n = h.shape[1]
        q, k, v, g = jnp.split(h @ w_in, [ATTN_WIDTH, ATTN_WIDTH + ATTN_KV_WIDTH,
                                          ATTN_WIDTH + 2 * ATTN_KV_WIDTH], axis=-1)
        q = q.reshape(bsz, n, ATTN_KV_HEADS, ATTN_GROUP, ATTN_HEAD_DIM)
        k = k.reshape(bsz, n, ATTN_KV_HEADS, ATTN_HEAD_DIM)
        v = v.reshape(bsz, n, ATTN_KV_HEADS, ATTN_HEAD_DIM)
        return q, k, v, g

    q_c, k_c, v_c, g_c = project(h_ctx)
    q, k, v, g = project(h_lat)
    q = axial_rope(q, row, col)
    k = axial_rope(k, row, col)
    scale = ATTN_HEAD_DIM ** -0.5
    sink_kg = sink.reshape(ATTN_KV_HEADS, ATTN_GROUP)

    n_blocks = L // BLOCK
    pad = jnp.zeros((bsz, BLOCK, ATTN_KV_HEADS, ATTN_HEAD_DIM), k.dtype)
    k_pad = jnp.concatenate([pad, k, pad], axis=1)
    v_pad = jnp.concatenate([pad, v, pad], axis=1)
    ctx_mask = jnp.ones((BLOCK, CTX_LEN), dtype=bool)

    def block(j):
        start = j * BLOCK
        qb = lax.dynamic_slice_in_dim(q, start, BLOCK, axis=1)
        kb = lax.dynamic_slice_in_dim(k_pad, start, 3 * BLOCK, axis=1)
        vb = lax.dynamic_slice_in_dim(v_pad, start, 3 * BLOCK, axis=1)
        kb = jnp.concatenate([kb, k_c], axis=1)
        vb = jnp.concatenate([vb, v_c], axis=1)
        s = jnp.einsum('bqkgd,bskd->bkgqs', qb, kb).astype(jnp.float32) * scale
        qpos = start + jnp.arange(BLOCK)
        kpos = start - BLOCK + jnp.arange(3 * BLOCK)
        band = ((kpos[None, :] >= 0) & (kpos[None, :] < L)
                & (jnp.abs(qpos[:, None] - kpos[None, :]) <= WINDOW))
        mask = jnp.concatenate([band, ctx_mask], axis=1)
        p = sink_softmax(s, sink_kg, mask).astype(vb.dtype)
        return jnp.einsum('bkgqs,bskd->bqkgd', p, vb)

    o = lax.map(block, jnp.arange(n_blocks))
    o = jnp.moveaxis(o, 0, 1).reshape(bsz, L, ATTN_WIDTH)
    y_lat = ((o * jax.nn.silu(g)) @ w_out).astype(h_lat.dtype)

    y_ctx = None
    if need_ctx_out:
        s = jnp.einsum('bqkgd,bskd->bkgqs', q_c, k_c).astype(jnp.float32) * scale
        p = sink_softmax(s, sink_kg, None).astype(v_c.dtype)
        o_c = jnp.einsum('bkgqs,bskd->bqkgd', p, v_c).reshape(bsz, CTX_LEN, ATTN_WIDTH)
        y_ctx = ((o_c * jax.nn.silu(g_c)) @ w_out).astype(h_ctx.dtype)
    return y_lat, y_ctx


def centred_depthwise_conv(u, w, b):
    out = lax.conv_general_dilated(u, w[:, None, :].astype(u.dtype), window_strides=(1,),
                                   padding=[(SSD_CONV // 2, SSD_CONV // 2)],
                                   dimension_numbers=('NWC', 'WIO', 'NWC'),
                                   feature_group_count=u.shape[-1])
    return out + b


def ssd_chunk_pass(xdt, da, bm, cm, h0, want_y):
    bsz, L = xdt.shape[:2]
    nc = L // SSD_CHUNK
    xdt = xdt.reshape(bsz, nc, SSD_CHUNK, SSD_GROUPS, SSD_HEADS_PER_GROUP, SSD_HEAD_DIM)
    da = da.reshape(bsz, nc, SSD_CHUNK, SSD_GROUPS, SSD_HEADS_PER_GROUP)
    bm = bm.reshape(bsz, nc, SSD_CHUNK, SSD_GROUPS, SSD_STATE)
    cm = cm.reshape(bsz, nc, SSD_CHUNK, SSD_GROUPS, SSD_STATE)
    a_cs = jnp.cumsum(da, axis=2)
    a_last = a_cs[:, :, -1]
    decay_to_end = jnp.exp(a_last[:, :, None] - a_cs)
    states = jnp.einsum('bcsgn,bcsgr,bcsgrp->bcgrpn', bm, decay_to_end, xdt)

    def step(h, inp):
        st, al = inp
        return jnp.exp(al)[..., None, None] * h + st, h

    h_final, h_prev = lax.scan(step, h0, (jnp.moveaxis(states, 1, 0), jnp.moveaxis(a_last, 1, 0)))
    if not want_y:
        return None, h_final
    h_prev = jnp.moveaxis(h_prev, 0, 1)
    diff = a_cs[:, :, :, None] - a_cs[:, :, None, :]
    lower = jnp.tril(jnp.ones((SSD_CHUNK, SSD_CHUNK), dtype=bool))[:, :, None, None]
    lmat = jnp.exp(jnp.where(lower, diff, -jnp.inf))
    cb = jnp.einsum('bcqgn,bcsgn->bcqsg', cm, bm)
    y_diag = jnp.einsum('bcqsg,bcqsgr,bcsgrp->bcqgrp', cb, lmat, xdt)
    y_off = jnp.einsum('bcqgn,bcgrpn,bcqgr->bcqgrp', cm, h_prev, jnp.exp(a_cs))
    y = (y_diag + y_off).reshape(bsz, L, SSD_GROUPS, SSD_HEADS_PER_GROUP, SSD_HEAD_DIM)
    return y, h_final


def ssd_mixer(h_lat, h_ctx, w_in, conv_w, conv_b, dt_bias, a_log, d_skip, norm_w, w_out,
              need_ctx_out):
    bsz = h_lat.shape[0]
    f32 = jnp.float32
    a_dir = -jnp.exp(a_log.astype(f32))
    dt_bias_dir = dt_bias.astype(f32).reshape(2, SSD_GROUPS, SSD_HEADS_PER_GROUP)

    def front(h):
        n = h.shape[1]
        z, xbc, dt_raw = jnp.split(h @ w_in, [SSD_WIDTH, SSD_WIDTH + SSD_CONV_DIM], axis=-1)
        xbc = jax.nn.silu(centred_depthwise_conv(xbc, conv_w, conv_b))
        xs, bm, cm = jnp.split(xbc, [SSD_WIDTH, SSD_WIDTH + SSD_GROUPS * SSD_STATE], axis=-1)
        xs = xs.reshape(bsz, n, SSD_GROUPS, SSD_HEADS_PER_GROUP, SSD_HEAD_DIM).astype(f32)
        bm = bm.reshape(bsz, n, SSD_GROUPS, SSD_STATE).astype(f32)
        cm = cm.reshape(bsz, n, SSD_GROUPS, SSD_STATE).astype(f32)
        dt = jax.nn.softplus(dt_raw.astype(f32).reshape(bsz, n, 2, SSD_GROUPS, SSD_HEADS_PER_GROUP)
                             + dt_bias_dir)
        return z, xs, bm, cm, dt

    def scan_dir(xs, bm, cm, dt, d, h0, want_y):
        dt_d = dt[:, :, d]
        xdt = xs * dt_d[..., None]
        da = dt_d * a_dir[d].reshape(SSD_GROUPS, SSD_HEADS_PER_GROUP)
        if d == 1:
            xdt, da, bm, cm = (jnp.flip(t, axis=1) for t in (xdt, da, bm, cm))
        y, h_fin = ssd_chunk_pass(xdt, da, bm, cm, h0, want_y)
        if want_y and d == 1:
            y = jnp.flip(y, axis=1)
        return y, h_fin

    def back(y_f, y_b, xs, z):
        n = xs.shape[1]
        y = y_f + y_b + d_skip.astype(f32).reshape(SSD_GROUPS, SSD_HEADS_PER_GROUP, 1) * xs
        u = y.reshape(bsz, n, SSD_WIDTH) * jax.nn.silu(z.astype(f32))
        u = rms_normalize(u.reshape(bsz, n, SSD_GROUPS, SSD_WIDTH // SSD_GROUPS))
        return (u.reshape(bsz, n, SSD_WIDTH) * norm_w) @ w_out

    z_c, xs_c, b_c, c_c, dt_c = front(h_ctx)
    z, xs, bm, cm, dt = front(h_lat)
    h0 = jnp.zeros((bsz, SSD_GROUPS, SSD_HEADS_PER_GROUP, SSD_HEAD_DIM, SSD_STATE), f32)
    y_c_f, s_f = scan_dir(xs_c, b_c, c_c, dt_c, 0, h0, need_ctx_out)
    y_c_b, s_b = scan_dir(xs_c, b_c, c_c, dt_c, 1, h0, need_ctx_out)
    y_f, _ = scan_dir(xs, bm, cm, dt, 0, s_f, True)
    y_b, _ = scan_dir(xs, bm, cm, dt, 1, s_b, True)
    y_lat = back(y_f, y_b, xs, z).astype(h_lat.dtype)
    y_ctx = back(y_c_f, y_c_b, xs_c, z_c).astype(h_ctx.dtype) if need_ctx_out else None
    return y_lat, y_ctx


def setup_inputs(seed: int = 0) -> dict:
    key = jax.random.key(seed)
    ks = jax.random.split(key, 20)
    f32 = jnp.float32
    nrm = lambda k, shape, s: jax.random.normal(k, shape, f32) * s
    dt0 = jnp.exp(jax.random.uniform(ks[12], (N_SSD_LAYERS, 2, SSD_HEADS), f32,
                                     math.log(1e-3), math.log(1e-1)))
    return {
        'x': nrm(ks[0], (BATCH, SEQ, D_MODEL), 1.0),
        'c': nrm(ks[1], (BATCH, D_MODEL), 1.0),
        'ctx': nrm(ks[2], (BATCH, CTX_LEN, D_MODEL), 1.0),
        'c_ctx': nrm(ks[3], (D_MODEL,), 1.0),
        'w_ada': nrm(ks[4], (DEPTH, D_MODEL, 3 * D_MODEL), 0.5 * D_MODEL ** -0.5),
        'b_ada': nrm(ks[5], (DEPTH, 3 * D_MODEL), 0.02),
        'attn_w_in': nrm(ks[6], (N_ATTN_LAYERS, D_MODEL, ATTN_IN), D_MODEL ** -0.5),
        'attn_sink': nrm(ks[7], (N_ATTN_LAYERS, ATTN_HEADS), 0.5),
        'attn_w_out': nrm(ks[8], (N_ATTN_LAYERS, ATTN_WIDTH, D_MODEL), ATTN_WIDTH ** -0.5),
        'ssd_w_in': nrm(ks[9], (N_SSD_LAYERS, D_MODEL, SSD_IN), D_MODEL ** -0.5),
        'ssd_conv_w': nrm(ks[10], (N_SSD_LAYERS, SSD_CONV, SSD_CONV_DIM), SSD_CONV ** -0.5),
        'ssd_conv_b': nrm(ks[11], (N_SSD_LAYERS, SSD_CONV_DIM), 0.02),
        'ssd_dt_bias': dt0 + jnp.log(-jnp.expm1(-dt0)),
        'ssd_a_log': jnp.log(jax.random.uniform(ks[13], (N_SSD_LAYERS, 2, SSD_HEADS), f32, 1.0, 16.0)),
        'ssd_d': 1.0 + nrm(ks[14], (N_SSD_LAYERS, SSD_HEADS), 0.1),
        'ssd_norm_w': 1.0 + nrm(ks[15], (N_SSD_LAYERS, SSD_WIDTH), 0.1),
        'ssd_w_out': nrm(ks[16], (N_SSD_LAYERS, SSD_WIDTH, D_MODEL), SSD_WIDTH ** -0.5),
        'final_norm_w': 1.0 + nrm(ks[17], (D_MODEL,), 0.1),
    }


def reference(x, c, ctx, c_ctx, w_ada, b_ada, attn_w_in, attn_sink, attn_w_out, ssd_w_in,
              ssd_conv_w, ssd_conv_b, ssd_dt_bias, ssd_a_log, ssd_d, ssd_norm_w, ssd_w_out,
              final_norm_w):
    L = x.shape[1]
    ROWS = L // GRID_W
    row = jnp.repeat(jnp.arange(ROWS, dtype=jnp.int32), GRID_W)
    col = jnp.tile(jnp.arange(GRID_W, dtype=jnp.int32), ROWS)
    h, hc = x, ctx
    for i in range(DEPTH):
        need_ctx_out = i < DEPTH - 1
        shift, scale, gate = adaln(c, w_ada[i], b_ada[i])
        shift_c, scale_c, gate_c = adaln(c_ctx, w_ada[i], b_ada[i])
        u = rms_normalize(h) * (1.0 + scale[:, None]) + shift[:, None]
        uc = rms_normalize(hc) * (1.0 + scale_c) + shift_c
        j = i // N_MIXERS
        if i % N_MIXERS == 0:
            y, yc = attention_mixer(u, uc, attn_w_in[j], attn_sink[j], attn_w_out[j],
                                    row, col, need_ctx_out)
        else:
            y, yc = ssd_mixer(u, uc, ssd_w_in[j], ssd_conv_w[j], ssd_conv_b[j], ssd_dt_bias[j],
                              ssd_a_log[j], ssd_d[j], ssd_norm_w[j], ssd_w_out[j], need_ctx_out)
        h = h + gate[:, None] * y
        if need_ctx_out:
            hc = hc + gate_c * yc
    return rms_normalize(h) * final_norm_w
```

```python
import functools
import math

import jax
import jax.numpy as jnp
from jax import lax
from jax.experimental import pallas as pl
from jax.experimental.pallas import tpu as pltpu

F32 = jnp.float32
BF16 = jnp.bfloat16

EPS = 1e-6
CTX_LEN = 256
GRID_W = 64
ROPE_BASE = 10000.0

HEAD_DIM = 64
ATTN_HEADS = 16
ATTN_KV_HEADS = 4
ATTN_GROUP = ATTN_HEADS // ATTN_KV_HEADS
ATTN_BLOCK = 128
LANES = 128
SUBLANES = 8

SSD_HEADS = 32
SSD_GROUPS = 8
SSD_HEADS_PER_GROUP = SSD_HEADS // SSD_GROUPS
SSD_STATE = 128
SSD_CONV = 5
SSD_CHUNK = 128
SSD_GROUP_WIDTH = SSD_HEADS_PER_GROUP * HEAD_DIM

ROW_TILE = 256
CONV_COL_TILE = 512
MASKED = -1e30
VMEM_LIMIT = 56 * 1024 * 1024


def _silu(x):
    return x / (1.0 + jnp.exp(-x))


def _norm_modulate(x, mod_ref):
    ms = jnp.mean(x * x, axis=-1, keepdims=True)
    return x * lax.rsqrt(ms + EPS) * (1.0 + mod_ref[1:2, :]) + mod_ref[0:1, :]


def _mod_row(is_ctx, b):
    return jnp.where(is_ctx, 8, b)


def _mod_kernel(cond_ref, w_ref, b_ref, o_ref):
    cnd = cond_ref[...]
    o_ref[...] = jnp.dot(_silu(cnd), w_ref[...], preferred_element_type=F32,
                         precision=lax.Precision.HIGHEST) + b_ref[...]


def _modulation_tables(cond, w_ada, b_ada):
    depth, d, d3 = w_ada.shape
    return pl.pallas_call(
        _mod_kernel,
        grid=(depth, d3 // d),
        in_specs=[
            pl.BlockSpec((16, d), lambda l, j: (0, 0)),
            pl.BlockSpec((None, d, d), lambda l, j: (l, 0, j)),
            pl.BlockSpec((None, 1, d), lambda l, j: (l, 0, j)),
        ],
        out_specs=pl.BlockSpec((None, 16, d), lambda l, j: (l, 0, j)),
        out_shape=jax.ShapeDtypeStruct((depth, 16, d3), F32),
        compiler_params=pltpu.CompilerParams(vmem_limit_bytes=VMEM_LIMIT),
        name="adaln_tables",
    )(cond, w_ada, b_ada.reshape(depth, 1, d3))


def _attn_in_kernel(h_ref, mod_ref, w_ref, cos_ref, sa_ref, sb_ref, q_ref, k_ref, v_ref, g_ref):
    d = h_ref.shape[-1]
    kvw = k_ref.shape[-1]
    ub = _norm_modulate(h_ref[...], mod_ref).astype(BF16)
    cos = cos_ref[...]
    sa = sa_ref[...]
    sb = sb_ref[...]

    def rope(t):
        return t * cos + pltpu.roll(t, LANES - 16, 1) * sa + pltpu.roll(t, 16, 1) * sb

    q = jnp.dot(ub, w_ref[:, 0:d], preferred_element_type=F32)
    scale = HEAD_DIM ** -0.5
    for j in range(d // LANES):
        sl = slice(j * LANES, (j + 1) * LANES)
        q_ref[:, sl] = (rope(q[:, sl]) * scale).astype(BF16)
    k = jnp.dot(ub, w_ref[:, d:d + kvw], preferred_element_type=F32)
    for j in range(kvw // LANES):
        sl = slice(j * LANES, (j + 1) * LANES)
        k_ref[:, sl] = rope(k[:, sl]).astype(BF16)
    v_ref[...] = jnp.dot(ub, w_ref[:, d + kvw:d + 2 * kvw], preferred_element_type=F32).astype(BF16)
    g = jnp.dot(ub, w_ref[:, d + 2 * kvw:], preferred_element_type=F32)
    g_ref[...] = _silu(g).astype(BF16)


def _attn_in(hs, mods, layer, w_in, cos, sa, sb):
    bsz, t, d = hs.shape
    kvw = ATTN_KV_HEADS * HEAD_DIM
    n_in = w_in.shape[1]
    ctx_tiles = CTX_LEN // ROW_TILE
    row_spec = lambda width: pl.BlockSpec((None, ROW_TILE, width), lambda b, i: (b, i, 0))
    tab_spec = pl.BlockSpec((ROW_TILE, LANES), lambda b, i: (i, 0))
    return pl.pallas_call(
        _attn_in_kernel,
        grid=(bsz, t // ROW_TILE),
        in_specs=[
            row_spec(d),
            pl.BlockSpec((None, None, 3, d), lambda b, i: (layer, _mod_row(i < ctx_tiles, b), 0, 0)),
            pl.BlockSpec((d, n_in), lambda b, i: (0, 0)),
            tab_spec, tab_spec, tab_spec,
        ],
        out_specs=[row_spec(d), row_spec(kvw), row_spec(kvw), row_spec(d)],
        out_shape=[
            jax.ShapeDtypeStruct((bsz, t, d), BF16),
            jax.ShapeDtypeStruct((bsz, t, kvw), BF16),
            jax.ShapeDtypeStruct((bsz, t, kvw), BF16),
            jax.ShapeDtypeStruct((bsz, t, d), BF16),
        ],
        compiler_params=pltpu.CompilerParams(
            dimension_semantics=("parallel", "parallel"), vmem_limit_bytes=VMEM_LIMIT),
        name="attn_in",
    )(hs, mods, w_in, cos, sa, sb)


def _attn_core_kernel(n_lat_blocks, sink_ref, q_ref, kp_ref, kc_ref, kn_ref, kx_ref,
                      vp_ref, vc_ref, vn_ref, vx_ref, g_ref, h_ref, mod_ref, wo_ref, o_ref,
                      kcat, vcat, ocat):
    blk = pl.program_id(1)
    ctx_blocks = CTX_LEN // ATTN_BLOCK
    j = blk - ctx_blocks
    is_lat = blk >= ctx_blocks
    nb = ATTN_BLOCK

    kcat[0:nb, :] = kp_ref[...]
    kcat[nb:2 * nb, :] = kc_ref[...]
    kcat[2 * nb:3 * nb, :] = kn_ref[...]
    kcat[3 * nb:, :] = kx_ref[...]
    vcat[0:nb, :] = vp_ref[...]
    vcat[nb:2 * nb, :] = vc_ref[...]
    vcat[2 * nb:3 * nb, :] = vn_ref[...]
    vcat[3 * nb:, :] = vx_ref[...]

    row = lax.broadcasted_iota(jnp.int32, (nb, nb), 0)
    col = lax.broadcasted_iota(jnp.int32, (nb, nb), 1)
    ok_prev = (col >= row) & (is_lat & (j > 0))
    ok_next = (col <= row) & (is_lat & (j < n_lat_blocks - 1))
    bias = jnp.concatenate(
        [jnp.where(ok_prev, 0.0, MASKED),
         jnp.broadcast_to(jnp.where(is_lat, 0.0, MASKED), (nb, nb)),
         jnp.where(ok_next, 0.0, MASKED),
         jnp.zeros((nb, CTX_LEN), F32)], axis=1).astype(F32)

    for h in range(ATTN_HEADS):
        kh = h // ATTN_GROUP
        hs_ = slice(h * HEAD_DIM, (h + 1) * HEAD_DIM)
        ks_ = slice(kh * HEAD_DIM, (kh + 1) * HEAD_DIM)
        s = lax.dot_general(q_ref[:, hs_], kcat[:, ks_], (((1,), (1,)), ((), ())),
                            preferred_element_type=F32) + bias
        sink = sink_ref[h]
        m = jnp.maximum(jnp.max(s, axis=-1, keepdims=True), sink)
        p = jnp.exp(s - m)
        den = jnp.sum(p, axis=-1, keepdims=True) + jnp.exp(sink - m)
        oh = jnp.dot(p.astype(BF16), vcat[:, ks_], preferred_element_type=F32)
        ocat[:, hs_] = oh / den

    og = (ocat[...] * g_ref[...].astype(F32)).astype(BF16)
    y = jnp.dot(og, wo_ref[...], preferred_element_type=F32)
    o_ref[...] = h_ref[...] + mod_ref[2:3, :] * y


def _attn_core(hs, mods, layer, q, k, v, g, sink, w_out):
    bsz, t, d = hs.shape
    kvw = ATTN_KV_HEADS * HEAD_DIM
    nb = ATTN_BLOCK
    ctx_blocks = CTX_LEN // nb
    n_blocks = t // nb
    n_lat = n_blocks - ctx_blocks
    blk_spec = lambda width, imap: pl.BlockSpec((None, nb, width), imap)
    own = lambda b, i: (b, i, 0)
    prev = lambda b, i: (b, jnp.maximum(i - 1, ctx_blocks), 0)
    nxt = lambda b, i: (b, jnp.minimum(jnp.maximum(i + 1, ctx_blocks), n_blocks - 1), 0)
    ctx_spec = pl.BlockSpec((None, CTX_LEN, kvw), lambda b, i: (b, 0, 0))
    kv_specs = [blk_spec(kvw, prev), blk_spec(kvw, own), blk_spec(kvw, nxt), ctx_spec]
    return pl.pallas_call(
        functools.partial(_attn_core_kernel, n_lat),
        grid=(bsz, n_blocks),
        in_specs=[
            pl.BlockSpec(memory_space=pltpu.SMEM),
            blk_spec(d, own),
            *kv_specs, *kv_specs,
            blk_spec(d, own),
            blk_spec(d, own),
            pl.BlockSpec((None, None, 3, d), lambda b, i: (layer, _mod_row(i < ctx_blocks, b), 0, 0)),
            pl.BlockSpec((d, d), lambda b, i: (0, 0)),
        ],
        out_specs=blk_spec(d, own),
        out_shape=jax.ShapeDtypeStruct((bsz, t, d), F32),
        scratch_shapes=[
            pltpu.VMEM((3 * nb + CTX_LEN, kvw), BF16),
            pltpu.VMEM((3 * nb + CTX_LEN, kvw), BF16),
            pltpu.VMEM((nb, d), F32),
        ],
        compiler_params=pltpu.CompilerParams(
            dimension_semantics=("parallel", "parallel"), vmem_limit_bytes=VMEM_LIMIT),
        name="attn_core",
    )(sink, q, k, k, k, k, v, v, v, v, g, hs, mods, w_out)


def _ssd_in_kernel(h_ref, hp_ref, hn_ref, mod_ref, wz_ref, wx_ref, wdt_ref, cw_ref, cb_ref, dtb_ref,
                   z_ref, xs_ref, bm_ref, cm_ref, dt_ref):
    blk = pl.program_id(1)
    n_tiles = pl.num_programs(1)
    ctx_tiles = CTX_LEN // ROW_TILE
    halo = SUBLANES
    tm = ROW_TILE
    width = xs_ref.shape[-1]
    gs_width = bm_ref.shape[-1]

    u_main = _norm_modulate(h_ref[...], mod_ref)
    u_all = jnp.concatenate(
        [_norm_modulate(hp_ref[...], mod_ref), u_main, _norm_modulate(hn_ref[...], mod_ref)],
        axis=0).astype(BF16)
    ub = u_main.astype(BF16)

    z_ref[...] = jnp.dot(ub, wz_ref[...], preferred_element_type=F32).astype(BF16)
    dt_raw = jnp.dot(ub, wdt_ref[...], preferred_element_type=F32) + dtb_ref[...]
    dt_ref[...] = jnp.maximum(dt_raw, 0.0) + jnp.log1p(jnp.exp(-jnp.abs(dt_raw)))

    prev_ok = (blk > ctx_tiles) | ((blk > 0) & (blk < ctx_tiles))
    next_ok = ((blk >= ctx_tiles) & (blk < n_tiles - 1)) | (blk < ctx_tiles - 1)
    rid = lax.broadcasted_iota(jnp.int32, (tm + 2 * halo, 1), 0)
    keep = ((rid >= halo) | prev_ok) & ((rid < tm + halo) | next_ok)

    pad = SSD_CONV // 2
    for c in range(wx_ref.shape[-1] // CONV_COL_TILE):
        cs = slice(c * CONV_COL_TILE, (c + 1) * CONV_COL_TILE)
        pre = jnp.dot(u_all, wx_ref[:, cs], preferred_element_type=F32)
        pre = jnp.where(keep, pre, 0.0)
        acc = cb_ref[:, cs] + cw_ref[0:1, cs] * pre[halo - pad:halo - pad + tm, :]
        for kk in range(1, SSD_CONV):
            acc = acc + cw_ref[kk:kk + 1, cs] * pre[halo - pad + kk:halo - pad + kk + tm, :]
        out = _silu(acc).astype(BF16)
        lo = c * CONV_COL_TILE
        if lo < width:
            xs_ref[:, lo:lo + CONV_COL_TILE] = out
        elif lo < width + gs_width:
            bm_ref[:, lo - width:lo - width + CONV_COL_TILE] = out
        else:
            cm_ref[:, lo - width - gs_width:lo - width - gs_width + CONV_COL_TILE] = out


def _ssd_in(hs, mods, layer, w_z, w_xbc, w_dt, conv_w, conv_b, dt_bias):
    bsz, t, d = hs.shape
    width = w_z.shape[1]
    gs_width = SSD_GROUPS * SSD_STATE
    n_tiles = t // ROW_TILE
    ctx_tiles = CTX_LEN // ROW_TILE
    per_tile = ROW_TILE // SUBLANES
    n_halo_blocks = t // SUBLANES
    row_spec = lambda w_: pl.BlockSpec((None, ROW_TILE, w_), lambda b, i: (b, i, 0))
    full = lambda a: pl.BlockSpec(a.shape, lambda b, i: (0,) * a.ndim)
    return pl.pallas_call(
        _ssd_in_kernel,
        grid=(bsz, n_tiles),
        in_specs=[
            row_spec(d),
            pl.BlockSpec((None, SUBLANES, d), lambda b, i: (b, jnp.maximum(i * per_tile - 1, 0), 0)),
            pl.BlockSpec((None, SUBLANES, d),
                         lambda b, i: (b, jnp.minimum((i + 1) * per_tile, n_halo_blocks - 1), 0)),
            pl.BlockSpec((None, None, 3, d), lambda b, i: (layer, _mod_row(i < ctx_tiles, b), 0, 0)),
            full(w_z), full(w_xbc), full(w_dt), full(conv_w), full(conv_b), full(dt_bias),
        ],
        out_specs=[row_spec(width), row_spec(width), row_spec(gs_width), row_spec(gs_width),
                   row_spec(LANES)],
        out_shape=[
            jax.ShapeDtypeStruct((bsz, t, width), BF16),
            jax.ShapeDtypeStruct((bsz, t, width), BF16),
            jax.ShapeDtypeStruct((bsz, t, gs_width), BF16),
            jax.ShapeDtypeStruct((bsz, t, gs_width), BF16),
            jax.ShapeDtypeStruct((bsz, t, LANES), F32),
        ],
        compiler_params=pltpu.CompilerParams(
            dimension_semantics=("parallel", "parallel"), vmem_limit_bytes=VMEM_LIMIT),
        name="ssd_in",
    )(hs, hs, hs, mods, w_z, w_xbc, w_dt, conv_w, conv_b, dt_bias)


def _ssd_chunk(direction, xs_ref, bm_ref, cm_ref, dt_ref, alog_ref, st_ref, emit):
    q = SSD_CHUNK
    dt = dt_ref[...]
    da = dt * (-jnp.exp(alog_ref[...]))
    row = lax.broadcasted_iota(jnp.int32, (q, q), 0)
    col = lax.broadcasted_iota(jnp.int32, (q, q), 1)
    seen = (row >= col) if direction == 0 else (col >= row)
    a = jnp.dot(seen.astype(F32), da, preferred_element_type=F32,
                precision=lax.Precision.HIGHEST)
    a_t = a.T
    dt_t = dt.T
    last = q - 1 if direction == 0 else 0
    a_tot = a[last:last + 1, :]
    exp_a = jnp.exp(a)
    w_end = jnp.exp(a_tot - a) * dt

    for g in range(SSD_GROUPS):
        gsl = slice(g * SSD_STATE, (g + 1) * SSD_STATE)
        cmg = cm_ref[:, gsl]
        bmg = bm_ref[:, gsl]
        cb = lax.dot_general(cmg, bmg, (((1,), (1,)), ((), ())), preferred_element_type=F32)
        h_in = st_ref[g]
        y_off = jnp.dot(cmg, h_in.astype(BF16), preferred_element_type=F32)
        ys, xws, decs = [], [], []
        for r in range(SSD_HEADS_PER_GROUP):
            h = g * SSD_HEADS_PER_GROUP + r
            c = direction * SSD_HEADS + h
            xh = xs_ref[:, h * HEAD_DIM:(h + 1) * HEAD_DIM]
            lmat = jnp.exp(jnp.where(seen, a[:, c:c + 1] - a_t[c:c + 1, :], -jnp.inf))
            w = (cb * lmat * dt_t[c:c + 1, :]).astype(BF16)
            y_diag = jnp.dot(w, xh, preferred_element_type=F32)
            ys.append(y_diag + y_off[:, r * HEAD_DIM:(r + 1) * HEAD_DIM] * exp_a[:, c:c + 1])
            xws.append((xh.astype(F32) * w_end[:, c:c + 1]).astype(BF16))
            decs.append(jnp.broadcast_to(jnp.exp(a_tot[:, c:c + 1]), (1, HEAD_DIM)))
        emit(g, jnp.concatenate(ys, axis=1))
        new = lax.dot_general(bmg, jnp.concatenate(xws, axis=1), (((0,), (0,)), ((), ())),
                              preferred_element_type=F32)
        st_ref[g] = h_in * jnp.concatenate(decs, axis=1) + new


def _ssd_fwd_kernel(xs_ref, bm_ref, cm_ref, dt_ref, alog_ref, dsk_ref, y_ref, st_ref):
    @pl.when(pl.program_id(1) == 0)
    def _():
        st_ref[...] = jnp.zeros_like(st_ref)

    def emit(g, y):
        sl = slice(g * SSD_GROUP_WIDTH, (g + 1) * SSD_GROUP_WIDTH)
        y_ref[:, sl] = y + dsk_ref[:, sl] * xs_ref[:, sl].astype(F32)

    _ssd_chunk(0, xs_ref, bm_ref, cm_ref, dt_ref, alog_ref, st_ref, emit)


def _ssd_bwd_kernel(xs_ref, bm_ref, cm_ref, dt_ref, alog_ref, yf_ref, z_ref, h_ref, mod_ref, nw_ref,
                    wo_ref, o_ref, st_ref, u_ref):
    @pl.when(pl.program_id(1) == 0)
    def _():
        st_ref[...] = jnp.zeros_like(st_ref)

    def emit(g, y):
        sl = slice(g * SSD_GROUP_WIDTH, (g + 1) * SSD_GROUP_WIDTH)
        u = (y + yf_ref[:, sl]) * _silu(z_ref[:, sl].astype(F32))
        ms = jnp.mean(u * u, axis=-1, keepdims=True)
        u_ref[:, sl] = (u * lax.rsqrt(ms + EPS) * nw_ref[:, sl]).astype(BF16)

    _ssd_chunk(1, xs_ref, bm_ref, cm_ref, dt_ref, alog_ref, st_ref, emit)
    y = jnp.dot(u_ref[...], wo_ref[...], preferred_element_type=F32)
    o_ref[...] = h_ref[...] + mod_ref[2:3, :] * y


def _ssd_scans(hs, mods, layer, z, xs, bm, cm, dt, a_log, d_skip, norm_w, w_out):
    bsz, t, d = hs.shape
    width = xs.shape[-1]
    gs_width = bm.shape[-1]
    q = SSD_CHUNK
    n_chunks = t // q
    ctx_chunks = CTX_LEN // q
    fwd = lambda b, i: (b, i, 0)
    bwd_chunk = lambda i: jnp.where(i < ctx_chunks, ctx_chunks - 1 - i, n_chunks - 1 + ctx_chunks - i)
    bwd = lambda b, i: (b, bwd_chunk(i), 0)
    const = lambda a: pl.BlockSpec(a.shape, lambda b, i: (0,) * a.ndim)
    state = pltpu.VMEM((SSD_GROUPS, SSD_STATE, SSD_GROUP_WIDTH), F32)
    chunk_specs = lambda imap: [
        pl.BlockSpec((None, q, width), imap), pl.BlockSpec((None, q, gs_width), imap),
        pl.BlockSpec((None, q, gs_width), imap), pl.BlockSpec((None, q, LANES), imap)]

    y_f = pl.pallas_call(
        _ssd_fwd_kernel,
        grid=(bsz, n_chunks),
        in_specs=[*chunk_specs(fwd), const(a_log), const(d_skip)],
        out_specs=pl.BlockSpec((None, q, width), fwd),
        out_shape=jax.ShapeDtypeStruct((bsz, t, width), F32),
        scratch_shapes=[state],
        compiler_params=pltpu.CompilerParams(
            dimension_semantics=("parallel", "arbitrary"), vmem_limit_bytes=VMEM_LIMIT),
        name="ssd_fwd",
    )(xs, bm, cm, dt, a_log, d_skip)

    return pl.pallas_call(
        _ssd_bwd_kernel,
        grid=(bsz, n_chunks),
        in_specs=[
            *chunk_specs(bwd), const(a_log),
            pl.BlockSpec((None, q, width), bwd),
            pl.BlockSpec((None, q, width), bwd),
            pl.BlockSpec((None, q, d), bwd),
            pl.BlockSpec((None, None, 3, d),
                         lambda b, i: (layer, _mod_row(bwd_chunk(i) < ctx_chunks, b), 0, 0)),
            const(norm_w), const(w_out),
        ],
        out_specs=pl.BlockSpec((None, q, d), bwd),
        out_shape=jax.ShapeDtypeStruct((bsz, t, d), F32),
        scratch_shapes=[state, pltpu.VMEM((q, width), BF16)],
        compiler_params=pltpu.CompilerParams(
            dimension_semantics=("parallel", "arbitrary"), vmem_limit_bytes=VMEM_LIMIT),
        name="ssd_bwd",
    )(xs, bm, cm, dt, a_log, y_f, z, hs, mods, norm_w, w_out)


def _final_norm_kernel(h_ref, w_ref, o_ref):
    x = h_ref[...]
    ms = jnp.mean(x * x, axis=-1, keepdims=True)
    o_ref[...] = x * lax.rsqrt(ms + EPS) * w_ref[...]


def _final_norm(hs, w):
    bsz, t, d = hs.shape
    ctx_tiles = CTX_LEN // ROW_TILE
    n_lat = t - CTX_LEN
    return pl.pallas_call(
        _final_norm_kernel,
        grid=(bsz, n_lat // ROW_TILE),
        in_specs=[pl.BlockSpec((None, ROW_TILE, d), lambda b, i: (b, i + ctx_tiles, 0)),
                  pl.BlockSpec((1, d), lambda b, i: (0, 0))],
        out_specs=pl.BlockSpec((None, ROW_TILE, d), lambda b, i: (b, i, 0)),
        out_shape=jax.ShapeDtypeStruct((bsz, n_lat, d), F32),
        compiler_params=pltpu.CompilerParams(
            dimension_semantics=("parallel", "parallel"), vmem_limit_bytes=VMEM_LIMIT),
        name="final_norm",
    )(hs, w)


def _rope_tables(seq_len):
    quarter = HEAD_DIM // 4
    pos = jnp.arange(seq_len, dtype=jnp.int32)
    row = (pos // GRID_W).astype(F32)
    col = (pos % GRID_W).astype(F32)
    inv_freq = ROPE_BASE ** (-jnp.arange(0, 2 * quarter, 2, dtype=F32) / (2 * quarter))
    lane = jnp.arange(LANES)
    in_head = lane % HEAD_DIM
    p = jnp.where((in_head < HEAD_DIM // 2)[None, :], row[:, None], col[:, None])
    ang = p * inv_freq[lane % quarter][None, :]
    first = ((lane % (2 * quarter)) < quarter)[None, :]
    cos = jnp.cos(ang)
    sin = jnp.sin(ang)
    sa = jnp.where(first, -sin, 0.0)
    sb = jnp.where(first, 0.0, sin)
    ident = jnp.ones((CTX_LEN, LANES), F32)
    zeros = jnp.zeros((CTX_LEN, LANES), F32)
    return (jnp.concatenate([ident, cos], 0), jnp.concatenate([zeros, sa], 0),
            jnp.concatenate([zeros, sb], 0))


def kernel(x, c, ctx, c_ctx, w_ada, b_ada, attn_w_in, attn_sink, attn_w_out, ssd_w_in, ssd_conv_w,
           ssd_conv_b, ssd_dt_bias, ssd_a_log, ssd_d, ssd_norm_w, ssd_w_out, final_norm_w):
    bsz, seq_len, d = x.shape
    depth = w_ada.shape[0]
    assert ctx.shape[1] == CTX_LEN and CTX_LEN % ROW_TILE == 0 and seq_len % ROW_TILE == 0
    assert bsz <= 8

    cond = jnp.zeros((16, d), F32).at[:bsz].set(c).at[8].set(c_ctx)
    mods = _modulation_tables(cond, w_ada, b_ada).reshape(depth, 16, 3, d)
    cos, sa, sb = _rope_tables(seq_len)
    hs = jnp.concatenate([ctx, x], axis=1)

    ssd_width = ssd_w_out.shape[1]
    conv_dim = ssd_conv_w.shape[-1]
    pad_lanes = lambda a: jnp.pad(a, ((0, 0), (0, LANES - a.shape[-1])))

    for i in range(depth):
        j = i // 2
        if i % 2 == 0:
            q, k, v, g = _attn_in(hs, mods, i, attn_w_in[j].astype(BF16), cos, sa, sb)
            hs = _attn_core(hs, mods, i, q, k, v, g, attn_sink[j], attn_w_out[j].astype(BF16))
        else:
            w_in = ssd_w_in[j]
            w_z = w_in[:, :ssd_width].astype(BF16)
            w_xbc = w_in[:, ssd_width:ssd_width + conv_dim].astype(BF16)
            w_dt = pad_lanes(w_in[:, ssd_width + conv_dim:]).astype(BF16)
            z, xs, bm, cm, dt = _ssd_in(
                hs, mods, i, w_z, w_xbc, w_dt, ssd_conv_w[j], ssd_conv_b[j][None, :],
                pad_lanes(ssd_dt_bias[j].reshape(1, -1)))
            hs = _ssd_scans(
                hs, mods, i, z, xs, bm, cm, dt, pad_lanes(ssd_a_log[j].reshape(1, -1)),
                jnp.repeat(ssd_d[j], HEAD_DIM)[None, :], ssd_norm_w[j][None, :],
                ssd_w_out[j].astype(BF16))
    return _final_norm(hs, final_norm_w[None, :])
```

```python
import functools
import math

import jax
import jax.numpy as jnp
from jax import lax
from jax.experimental import pallas as pl
from jax.experimental.pallas import tpu as pltpu

F32 = jnp.float32
BF16 = jnp.bfloat16

EPS = 1e-6
CTX_LEN = 256
GRID_W = 64
ROPE_BASE = 10000.0

HEAD_DIM = 64
ATTN_HEADS = 16
ATTN_KV_HEADS = 4
ATTN_GROUP = ATTN_HEADS // ATTN_KV_HEADS
ATTN_BLOCK = 128
LANES = 128
SUBLANES = 8

SSD_HEADS = 32
SSD_GROUPS = 8
SSD_HEADS_PER_GROUP = SSD_HEADS // SSD_GROUPS
SSD_STATE = 128
SSD_CONV = 5
SSD_CHUNK = 128
SSD_GROUP_WIDTH = SSD_HEADS_PER_GROUP * HEAD_DIM

ROW_TILE = 256
CONV_COL_TILE = 512
MASKED = -1e30
VMEM_LIMIT = 56 * 1024 * 1024


def _silu(x):
    return x / (1.0 + jnp.exp(-x))


def _norm_modulate(x, mod_ref):
    ms = jnp.mean(x * x, axis=-1, keepdims=True)
    return x * lax.rsqrt(ms + EPS) * (1.0 + mod_ref[1:2, :]) + mod_ref[0:1, :]


def _mod_row(is_ctx, b):
    return jnp.where(is_ctx, 8, b)


def _mod_kernel(cond_ref, w_ref, b_ref, o_ref):
    cnd = cond_ref[...]
    o_ref[...] = jnp.dot(_silu(cnd), w_ref[...], preferred_element_type=F32,
                         precision=lax.Precision.HIGHEST) + b_ref[...]


def _modulation_tables(cond, w_ada, b_ada):
    depth, d, d3 = w_ada.shape
    return pl.pallas_call(
        _mod_kernel,
        grid=(depth, d3 // d),
        in_specs=[
            pl.BlockSpec((16, d), lambda l, j: (0, 0)),
            pl.BlockSpec((None, d, d), lambda l, j: (l, 0, j)),
            pl.BlockSpec((None, 1, d), lambda l, j: (l, 0, j)),
        ],
        out_specs=pl.BlockSpec((None, 16, d), lambda l, j: (l, 0, j)),
        out_shape=jax.ShapeDtypeStruct((depth, 16, d3), F32),
        compiler_params=pltpu.CompilerParams(vmem_limit_bytes=VMEM_LIMIT),
        name="adaln_tables",
    )(cond, w_ada, b_ada.reshape(depth, 1, d3))


def _attn_in_kernel(h_ref, mod_ref, w_ref, cos_ref, sa_ref, sb_ref, q_ref, k_ref, v_ref, g_ref):
    d = h_ref.shape[-1]
    kvw = k_ref.shape[-1]
    ub = _norm_modulate(h_ref[...], mod_ref).astype(BF16)
    cos = cos_ref[...]
    sa = sa_ref[...]
    sb = sb_ref[...]

    def rope(t):
        return t * cos + pltpu.roll(t, LANES - 16, 1) * sa + pltpu.roll(t, 16, 1) * sb

    q = jnp.dot(ub, w_ref[:, 0:d], preferred_element_type=F32)
    scale = HEAD_DIM ** -0.5
    for j in range(d // LANES):
        sl = slice(j * LANES, (j + 1) * LANES)
        q_ref[:, sl] = (rope(q[:, sl]) * scale).astype(BF16)
    k = jnp.dot(ub, w_ref[:, d:d + kvw], preferred_element_type=F32)
    for j in range(kvw // LANES):
        sl = slice(j * LANES, (j + 1) * LANES)
        k_ref[:, sl] = rope(k[:, sl]).astype(BF16)
    v_ref[...] = jnp.dot(ub, w_ref[:, d + kvw:d + 2 * kvw], preferred_element_type=F32).astype(BF16)
    g = jnp.dot(ub, w_ref[:, d + 2 * kvw:], preferred_element_type=F32)
    g_ref[...] = _silu(g).astype(BF16)


def _attn_in(hs, mods, layer, w_in, cos, sa, sb):
    bsz, t, d = hs.shape
    kvw = ATTN_KV_HEADS * HEAD_DIM
    n_in = w_in.shape[1]
    ctx_tiles = CTX_LEN // ROW_TILE
    row_spec = lambda width: pl.BlockSpec((None, ROW_TILE, width), lambda b, i: (b, i, 0))
    tab_spec = pl.BlockSpec((ROW_TILE, LANES), lambda b, i: (i, 0))
    return pl.pallas_call(
        _attn_in_kernel,
        grid=(bsz, t // ROW_TILE),
        in_specs=[
            row_spec(d),
            pl.BlockSpec((None, None, 3, d), lambda b, i: (layer, _mod_row(i < ctx_tiles, b), 0, 0)),
            pl.BlockSpec((d, n_in), lambda b, i: (0, 0)),
            tab_spec, tab_spec, tab_spec,
        ],
        out_specs=[row_spec(d), row_spec(kvw), row_spec(kvw), row_spec(d)],
        out_shape=[
            jax.ShapeDtypeStruct((bsz, t, d), BF16),
            jax.ShapeDtypeStruct((bsz, t, kvw), BF16),
            jax.ShapeDtypeStruct((bsz, t, kvw), BF16),
            jax.ShapeDtypeStruct((bsz, t, d), BF16),
        ],
        compiler_params=pltpu.CompilerParams(
            dimension_semantics=("parallel", "parallel"), vmem_limit_bytes=VMEM_LIMIT),
        name="attn_in",
    )(hs, mods, w_in, cos, sa, sb)


def _attn_core_kernel(n_lat_blocks, sink_ref, q_ref, kp_ref, kc_ref, kn_ref, kx_ref,
                      vp_ref, vc_ref, vn_ref, vx_ref, g_ref, h_ref, mod_ref, wo_ref, o_ref,
                      kcat, vcat, ocat):
    blk = pl.program_id(1)
    ctx_blocks = CTX_LEN // ATTN_BLOCK
    j = blk - ctx_blocks
    is_lat = blk >= ctx_blocks
    nb = ATTN_BLOCK

    kcat[0:nb, :] = kp_ref[...]
    kcat[nb:2 * nb, :] = kc_ref[...]
    kcat[2 * nb:3 * nb, :] = kn_ref[...]
    kcat[3 * nb:, :] = kx_ref[...]
    vcat[0:nb, :] = vp_ref[...]
    vcat[nb:2 * nb, :] = vc_ref[...]
    vcat[2 * nb:3 * nb, :] = vn_ref[...]
    vcat[3 * nb:, :] = vx_ref[...]

    row = lax.broadcasted_iota(jnp.int32, (nb, nb), 0)
    col = lax.broadcasted_iota(jnp.int32, (nb, nb), 1)
    ok_prev = (col >= row) & (is_lat & (j > 0))
    ok_next = (col <= row) & (is_lat & (j < n_lat_blocks - 1))
    bias = jnp.concatenate(
        [jnp.where(ok_prev, 0.0, MASKED),
         jnp.broadcast_to(jnp.where(is_lat, 0.0, MASKED), (nb, nb)),
         jnp.where(ok_next, 0.0, MASKED),
         jnp.zeros((nb, CTX_LEN), F32)], axis=1).astype(F32)

    for h in range(ATTN_HEADS):
        kh = h // ATTN_GROUP
        hs_ = slice(h * HEAD_DIM, (h + 1) * HEAD_DIM)
        ks_ = slice(kh * HEAD_DIM, (kh + 1) * HEAD_DIM)
        s = lax.dot_general(q_ref[:, hs_], kcat[:, ks_], (((1,), (1,)), ((), ())),
                            preferred_element_type=F32) + bias
        sink = sink_ref[h]
        m = jnp.maximum(jnp.max(s, axis=-1, keepdims=True), sink)
        p = jnp.exp(s - m)
        den = jnp.sum(p, axis=-1, keepdims=True) + jnp.exp(sink - m)
        oh = jnp.dot(p.astype(BF16), vcat[:, ks_], preferred_element_type=F32)
        ocat[:, hs_] = oh / den

    og = (ocat[...] * g_ref[...].astype(F32)).astype(BF16)
    y = jnp.dot(og, wo_ref[...], preferred_element_type=F32)
    o_ref[...] = h_ref[...] + mod_ref[2:3, :] * y


def _attn_core(hs, mods, layer, q, k, v, g, sink, w_out):
    bsz, t, d = hs.shape
    kvw = ATTN_KV_HEADS * HEAD_DIM
    nb = ATTN_BLOCK
    ctx_blocks = CTX_LEN // nb
    n_blocks = t // nb
    n_lat = n_blocks - ctx_blocks
    blk_spec = lambda width, imap: pl.BlockSpec((None, nb, width), imap)
    own = lambda b, i: (b, i, 0)
    prev = lambda b, i: (b, jnp.maximum(i - 1, ctx_blocks), 0)
    nxt = lambda b, i: (b, jnp.minimum(jnp.maximum(i + 1, ctx_blocks), n_blocks - 1), 0)
    ctx_spec = pl.BlockSpec((None, CTX_LEN, kvw), lambda b, i: (b, 0, 0))
    kv_specs = [blk_spec(kvw, prev), blk_spec(kvw, own), blk_spec(kvw, nxt), ctx_spec]
    return pl.pallas_call(
        functools.partial(_attn_core_kernel, n_lat),
        grid=(bsz, n_blocks),
        in_specs=[
            pl.BlockSpec(memory_space=pltpu.SMEM),
            blk_spec(d, own),
            *kv_specs, *kv_specs,
            blk_spec(d, own),
            blk_spec(d, own),
            pl.BlockSpec((None, None, 3, d), lambda b, i: (layer, _mod_row(i < ctx_blocks, b), 0, 0)),
            pl.BlockSpec((d, d), lambda b, i: (0, 0)),
        ],
        out_specs=blk_spec(d, own),
        out_shape=jax.ShapeDtypeStruct((bsz, t, d), F32),
        scratch_shapes=[
            pltpu.VMEM((3 * nb + CTX_LEN, kvw), BF16),
            pltpu.VMEM((3 * nb + CTX_LEN, kvw), BF16),
            pltpu.VMEM((nb, d), F32),
        ],
        compiler_params=pltpu.CompilerParams(
            dimension_semantics=("parallel", "parallel"), vmem_limit_bytes=VMEM_LIMIT),
        name="attn_core",
    )(sink, q, k, k, k, k, v, v, v, v, g, hs, mods, w_out)


def _ssd_in_kernel(h_ref, hp_ref, hn_ref, mod_ref, wz_ref, wx_ref, wdt_ref, cw_ref, cb_ref, dtb_ref,
                   z_ref, xs_ref, bm_ref, cm_ref, dt_ref):
    blk = pl.program_id(1)
    n_tiles = pl.num_programs(1)
    ctx_tiles = CTX_LEN // ROW_TILE
    halo = SUBLANES
    tm = ROW_TILE
    width = xs_ref.shape[-1]
    gs_width = bm_ref.shape[-1]

    u_main = _norm_modulate(h_ref[...], mod_ref)
    u_all = jnp.concatenate(
        [_norm_modulate(hp_ref[...], mod_ref), u_main, _norm_modulate(hn_ref[...], mod_ref)],
        axis=0).astype(BF16)
    ub = u_main.astype(BF16)

    z_ref[...] = jnp.dot(ub, wz_ref[...], preferred_element_type=F32).astype(BF16)
    dt_raw = jnp.dot(ub, wdt_ref[...], preferred_element_type=F32) + dtb_ref[...]
    dt_ref[...] = jnp.maximum(dt_raw, 0.0) + jnp.log1p(jnp.exp(-jnp.abs(dt_raw)))

    prev_ok = (blk > ctx_tiles) | ((blk > 0) & (blk < ctx_tiles))
    next_ok = ((blk >= ctx_tiles) & (blk < n_tiles - 1)) | (blk < ctx_tiles - 1)
    rid = lax.broadcasted_iota(jnp.int32, (tm + 2 * halo, 1), 0)
    keep = ((rid >= halo) | prev_ok) & ((rid < tm + halo) | next_ok)

    pad = SSD_CONV // 2
    for c in range(wx_ref.shape[-1] // CONV_COL_TILE):
        cs = slice(c * CONV_COL_TILE, (c + 1) * CONV_COL_TILE)
        pre = jnp.dot(u_all, wx_ref[:, cs], preferred_element_type=F32)
        pre = jnp.where(keep, pre, 0.0)
        acc = cb_ref[:, cs] + cw_ref[0:1, cs] * pre[halo - pad:halo - pad + tm, :]
        for kk in range(1, SSD_CONV):
            acc = acc + cw_ref[kk:kk + 1, cs] * pre[halo - pad + kk:halo - pad + kk + tm, :]
        out = _silu(acc).astype(BF16)
        lo = c * CONV_COL_TILE
        if lo < width:
            xs_ref[:, lo:lo + CONV_COL_TILE] = out
        elif lo < width + gs_width:
            bm_ref[:, lo - width:lo - width + CONV_COL_TILE] = out
        else:
            cm_ref[:, lo - width - gs_width:lo - width - gs_width + CONV_COL_TILE] = out


def _ssd_in(hs, mods, layer, w_z, w_xbc, w_dt, conv_w, conv_b, dt_bias):
    bsz, t, d = hs.shape
    width = w_z.shape[1]
    gs_width = SSD_GROUPS * SSD_STATE
    n_tiles = t // ROW_TILE
    ctx_tiles = CTX_LEN // ROW_TILE
    per_tile = ROW_TILE // SUBLANES
    n_halo_blocks = t // SUBLANES
    row_spec = lambda w_: pl.BlockSpec((None, ROW_TILE, w_), lambda b, i: (b, i, 0))
    full = lambda a: pl.BlockSpec(a.shape, lambda b, i: (0,) * a.ndim)
    return pl.pallas_call(
        _ssd_in_kernel,
        grid=(bsz, n_tiles),
        in_specs=[
            row_spec(d),
            pl.BlockSpec((None, SUBLANES, d), lambda b, i: (b, jnp.maximum(i * per_tile - 1, 0), 0)),
            pl.BlockSpec((None, SUBLANES, d),
                         lambda b, i: (b, jnp.minimum((i + 1) * per_tile, n_halo_blocks - 1), 0)),
            pl.BlockSpec((None, None, 3, d), lambda b, i: (layer, _mod_row(i < ctx_tiles, b), 0, 0)),
            full(w_z), full(w_xbc), full(w_dt), full(conv_w), full(conv_b), full(dt_bias),
        ],
        out_specs=[row_spec(width), row_spec(width), row_spec(gs_width), row_spec(gs_width),
                   row_spec(LANES)],
        out_shape=[
            jax.ShapeDtypeStruct((bsz, t, width), BF16),
            jax.ShapeDtypeStruct((bsz, t, width), BF16),
            jax.ShapeDtypeStruct((bsz, t, gs_width), BF16),
            jax.ShapeDtypeStruct((bsz, t, gs_width), BF16),
            jax.ShapeDtypeStruct((bsz, t, LANES), F32),
        ],
        compiler_params=pltpu.CompilerParams(
            dimension_semantics=("parallel", "parallel"), vmem_limit_bytes=VMEM_LIMIT),
        name="ssd_in",
    )(hs, hs, hs, mods, w_z, w_xbc, w_dt, conv_w, conv_b, dt_bias)


def _ssd_chunk(direction, xs_ref, bm_ref, cm_ref, dt_ref, alog_ref, st_ref, emit, skip_ref=None):
    q = SSD_CHUNK
    log2e = math.log2(math.e)
    dt = dt_ref[...]
    da = dt * (-jnp.exp(alog_ref[...]))
    row = lax.broadcasted_iota(jnp.int32, (q, q), 0)
    col = lax.broadcasted_iota(jnp.int32, (q, q), 1)
    seen = (row >= col) if direction == 0 else (col >= row)
    a = jnp.dot(seen.astype(F32), da, preferred_element_type=F32,
                precision=lax.Precision.HIGHEST)
    last = q - 1 if direction == 0 else 0
    a_tot = a[last:last + 1, :]
    w_end = jnp.exp(a_tot - a) * dt
    dec = jnp.exp(a_tot)
    a2 = a * log2e
    src2_t = ((a - jnp.log(dt)) * log2e).T
    left = col < HEAD_DIM

    for g in range(SSD_GROUPS):
        gsl = slice(g * SSD_STATE, (g + 1) * SSD_STATE)
        cmg = cm_ref[:, gsl]
        bmg = bm_ref[:, gsl]
        cb = lax.dot_general(cmg, bmg, (((1,), (1,)), ((), ())), preferred_element_type=F32)
        h_in = st_ref[g]
        y_off = jnp.dot(cmg, h_in.astype(BF16), preferred_element_type=F32)
        ys, xws, decs = [], [], []
        for pr in range(SSD_HEADS_PER_GROUP // 2):
            pair = g * (SSD_HEADS_PER_GROUP // 2) + pr
            c0 = direction * SSD_HEADS + 2 * pair
            psl = slice(pair * LANES, (pair + 1) * LANES)
            xp = xs_ref[:, psl]
            ws, eas = [], []
            for c in (c0, c0 + 1):
                a_col = jnp.broadcast_to(a2[:, c:c + 1], (q, q))
                lmat = jnp.exp2(jnp.where(seen, a_col - src2_t[c:c + 1, :], -jnp.inf))
                ws.append((cb * lmat).astype(BF16))
                eas.append(jnp.exp2(a_col))
            zero = jnp.zeros_like(xp)
            x_diag = jnp.concatenate([jnp.where(left, xp, zero), jnp.where(left, zero, xp)], axis=0)
            y_diag = jnp.dot(jnp.concatenate(ws, axis=1), x_diag, preferred_element_type=F32)
            y = y_diag + y_off[:, pr * LANES:(pr + 1) * LANES] * jnp.where(left, eas[0], eas[1])
            xf = xp.astype(F32)
            if skip_ref is not None:
                y = y + skip_ref[:, psl] * xf
            ys.append(y)
            idx = jnp.where(left, c0, c0 + 1)
            xws.append((xf * jnp.take_along_axis(w_end, idx, axis=1)).astype(BF16))
            decs.append(jnp.where(left[0:1, :], jnp.broadcast_to(dec[:, c0:c0 + 1], (1, LANES)),
                                  jnp.broadcast_to(dec[:, c0 + 1:c0 + 2], (1, LANES))))
        emit(g, jnp.concatenate(ys, axis=1))
        new = lax.dot_general(bmg, jnp.concatenate(xws, axis=1), (((0,), (0,)), ((), ())),
                              preferred_element_type=F32)
        st_ref[g] = h_in * jnp.concatenate(decs, axis=1) + new


def _ssd_fwd_kernel(xs_ref, bm_ref, cm_ref, dt_ref, alog_ref, dsk_ref, y_ref, st_ref):
    @pl.when(pl.program_id(1) == 0)
    def _():
        st_ref[...] = jnp.zeros_like(st_ref)

    def emit(g, y):
        y_ref[:, g * SSD_GROUP_WIDTH:(g + 1) * SSD_GROUP_WIDTH] = y

    _ssd_chunk(0, xs_ref, bm_ref, cm_ref, dt_ref, alog_ref, st_ref, emit, skip_ref=dsk_ref)


def _ssd_bwd_kernel(xs_ref, bm_ref, cm_ref, dt_ref, alog_ref, yf_ref, z_ref, h_ref, mod_ref, nw_ref,
                    wo_ref, o_ref, st_ref, u_ref):
    @pl.when(pl.program_id(1) == 0)
    def _():
        st_ref[...] = jnp.zeros_like(st_ref)

    def emit(g, y):
        sl = slice(g * SSD_GROUP_WIDTH, (g + 1) * SSD_GROUP_WIDTH)
        u = (y + yf_ref[:, sl]) * _silu(z_ref[:, sl].astype(F32))
        ms = jnp.mean(u * u, axis=-1, keepdims=True)
        u_ref[:, sl] = (u * lax.rsqrt(ms + EPS) * nw_ref[:, sl]).astype(BF16)

    _ssd_chunk(1, xs_ref, bm_ref, cm_ref, dt_ref, alog_ref, st_ref, emit)
    y = jnp.dot(u_ref[...], wo_ref[...], preferred_element_type=F32)
    o_ref[...] = h_ref[...] + mod_ref[2:3, :] * y


def _ssd_scans(hs, mods, layer, z, xs, bm, cm, dt, a_log, d_skip, norm_w, w_out):
    bsz, t, d = hs.shape
    width = xs.shape[-1]
    gs_width = bm.shape[-1]
    q = SSD_CHUNK
    n_chunks = t // q
    ctx_chunks = CTX_LEN // q
    fwd = lambda b, i: (b, i, 0)
    bwd_chunk = lambda i: jnp.where(i < ctx_chunks, ctx_chunks - 1 - i, n_chunks - 1 + ctx_chunks - i)
    bwd = lambda b, i: (b, bwd_chunk(i), 0)
    const = lambda a: pl.BlockSpec(a.shape, lambda b, i: (0,) * a.ndim)
    state = pltpu.VMEM((SSD_GROUPS, SSD_STATE, SSD_GROUP_WIDTH), F32)
    chunk_specs = lambda imap: [
        pl.BlockSpec((None, q, width), imap), pl.BlockSpec((None, q, gs_width), imap),
        pl.BlockSpec((None, q, gs_width), imap), pl.BlockSpec((None, q, LANES), imap)]

    y_f = pl.pallas_call(
        _ssd_fwd_kernel,
        grid=(bsz, n_chunks),
        in_specs=[*chunk_specs(fwd), const(a_log), const(d_skip)],
        out_specs=pl.BlockSpec((None, q, width), fwd),
        out_shape=jax.ShapeDtypeStruct((bsz, t, width), F32),
        scratch_shapes=[state],
        compiler_params=pltpu.CompilerParams(
            dimension_semantics=("parallel", "arbitrary"), vmem_limit_bytes=VMEM_LIMIT),
        name="ssd_fwd",
    )(xs, bm, cm, dt, a_log, d_skip)

    return pl.pallas_call(
        _ssd_bwd_kernel,
        grid=(bsz, n_chunks),
        in_specs=[
            *chunk_specs(bwd), const(a_log),
            pl.BlockSpec((None, q, width), bwd),
            pl.BlockSpec((None, q, width), bwd),
            pl.BlockSpec((None, q, d), bwd),
            pl.BlockSpec((None, None, 3, d),
                         lambda b, i: (layer, _mod_row(bwd_chunk(i) < ctx_chunks, b), 0, 0)),
            const(norm_w), const(w_out),
        ],
        out_specs=pl.BlockSpec((None, q, d), bwd),
        out_shape=jax.ShapeDtypeStruct((bsz, t, d), F32),
        scratch_shapes=[state, pltpu.VMEM((q, width), BF16)],
        compiler_params=pltpu.CompilerParams(
            dimension_semantics=("parallel", "arbitrary"), vmem_limit_bytes=VMEM_LIMIT),
        name="ssd_bwd",
    )(xs, bm, cm, dt, a_log, y_f, z, hs, mods, norm_w, w_out)


def _final_norm_kernel(h_ref, w_ref, o_ref):
    x = h_ref[...]
    ms = jnp.mean(x * x, axis=-1, keepdims=True)
    o_ref[...] = x * lax.rsqrt(ms + EPS) * w_ref[...]


def _final_norm(hs, w):
    bsz, t, d = hs.shape
    ctx_tiles = CTX_LEN // ROW_TILE
    n_lat = t - CTX_LEN
    return pl.pallas_call(
        _final_norm_kernel,
        grid=(bsz, n_lat // ROW_TILE),
        in_specs=[pl.BlockSpec((None, ROW_TILE, d), lambda b, i: (b, i + ctx_tiles, 0)),
                  pl.BlockSpec((1, d), lambda b, i: (0, 0))],
        out_specs=pl.BlockSpec((None, ROW_TILE, d), lambda b, i: (b, i, 0)),
        out_shape=jax.ShapeDtypeStruct((bsz, n_lat, d), F32),
        compiler_params=pltpu.CompilerParams(
            dimension_semantics=("parallel", "parallel"), vmem_limit_bytes=VMEM_LIMIT),
        name="final_norm",
    )(hs, w)


def _rope_tables(seq_len):
    quarter = HEAD_DIM // 4
    pos = jnp.arange(seq_len, dtype=jnp.int32)
    row = (pos // GRID_W).astype(F32)
    col = (pos % GRID_W).astype(F32)
    inv_freq = ROPE_BASE ** (-jnp.arange(0, 2 * quarter, 2, dtype=F32) / (2 * quarter))
    lane = jnp.arange(LANES)
    in_head = lane % HEAD_DIM
    p = jnp.where((in_head < HEAD_DIM // 2)[None, :], row[:, None], col[:, None])
    ang = p * inv_freq[lane % quarter][None, :]
    first = ((lane % (2 * quarter)) < quarter)[None, :]
    cos = jnp.cos(ang)
    sin = jnp.sin(ang)
    sa = jnp.where(first, -sin, 0.0)
    sb = jnp.where(first, 0.0, sin)
    ident = jnp.ones((CTX_LEN, LANES), F32)
    zeros = jnp.zeros((CTX_LEN, LANES), F32)
    return (jnp.concatenate([ident, cos], 0), jnp.concatenate([zeros, sa], 0),
            jnp.concatenate([zeros, sb], 0))


def kernel(x, c, ctx, c_ctx, w_ada, b_ada, attn_w_in, attn_sink, attn_w_out, ssd_w_in, ssd_conv_w,
           ssd_conv_b, ssd_dt_bias, ssd_a_log, ssd_d, ssd_norm_w, ssd_w_out, final_norm_w):
    bsz, seq_len, d = x.shape
    depth = w_ada.shape[0]
    assert ctx.shape[1] == CTX_LEN and CTX_LEN % ROW_TILE == 0 and seq_len % ROW_TILE == 0
    assert bsz <= 8

    cond = jnp.zeros((16, d), F32).at[:bsz].set(c).at[8].set(c_ctx)
    mods = _modulation_tables(cond, w_ada, b_ada).reshape(depth, 16, 3, d)
    cos, sa, sb = _rope_tables(seq_len)
    hs = jnp.concatenate([ctx, x], axis=1)

    ssd_width = ssd_w_out.shape[1]
    conv_dim = ssd_conv_w.shape[-1]
    pad_lanes = lambda a: jnp.pad(a, ((0, 0), (0, LANES - a.shape[-1])))

    for i in range(depth):
        j = i // 2
        if i % 2 == 0:
            q, k, v, g = _attn_in(hs, mods, i, attn_w_in[j].astype(BF16), cos, sa, sb)
            hs = _attn_core(hs, mods, i, q, k, v, g, attn_sink[j], attn_w_out[j].astype(BF16))
        else:
            w_in = ssd_w_in[j]
            w_z = w_in[:, :ssd_width].astype(BF16)
            w_xbc = w_in[:, ssd_width:ssd_width + conv_dim].astype(BF16)
            w_dt = pad_lanes(w_in[:, ssd_width + conv_dim:]).astype(BF16)
            z, xs, bm, cm, dt = _ssd_in(
                hs, mods, i, w_z, w_xbc, w_dt, ssd_conv_w[j], ssd_conv_b[j][None, :],
                pad_lanes(ssd_dt_bias[j].reshape(1, -1)))
            hs = _ssd_scans(
                hs, mods, i, z, xs, bm, cm, dt, pad_lanes(ssd_a_log[j].reshape(1, -1)),
                jnp.repeat(ssd_d[j], HEAD_DIM)[None, :], ssd_norm_w[j][None, :],
                ssd_w_out[j].astype(BF16))
    return _final_norm(hs, final_norm_w[None, :])
```

```python
import functools
import math

import jax
import jax.numpy as jnp
from jax import lax
from jax.experimental import pallas as pl
from jax.experimental.pallas import tpu as pltpu

F32 = jnp.float32
BF16 = jnp.bfloat16

EPS = 1e-6
CTX_LEN = 256
GRID_W = 64
ROPE_BASE = 10000.0

HEAD_DIM = 64
ATTN_HEADS = 16
ATTN_KV_HEADS = 4
ATTN_GROUP = ATTN_HEADS // ATTN_KV_HEADS
ATTN_BLOCK = 128
LANES = 128
SUBLANES = 8

SSD_HEADS = 32
SSD_GROUPS = 8
SSD_HEADS_PER_GROUP = SSD_HEADS // SSD_GROUPS
SSD_STATE = 128
SSD_CONV = 5
SSD_CHUNK = 128
SSD_GROUP_WIDTH = SSD_HEADS_PER_GROUP * HEAD_DIM

ROW_TILE = 256
CONV_COL_TILE = 512
MASKED = -1e30
VMEM_LIMIT = 56 * 1024 * 1024


def _silu(x):
    return x / (1.0 + jnp.exp(-x))


def _norm_modulate(x, mod_ref):
    ms = jnp.mean(x * x, axis=-1, keepdims=True)
    return x * lax.rsqrt(ms + EPS) * (1.0 + mod_ref[1:2, :]) + mod_ref[0:1, :]


def _mod_row(is_ctx, b):
    return jnp.where(is_ctx, 8, b)


def _mod_kernel(cond_ref, w_ref, b_ref, o_ref):
    cnd = cond_ref[...]
    o_ref[...] = jnp.dot(_silu(cnd), w_ref[...], preferred_element_type=F32,
                         precision=lax.Precision.HIGHEST) + b_ref[...]


def _modulation_tables(cond, w_ada, b_ada):
    depth, d, d3 = w_ada.shape
    return pl.pallas_call(
        _mod_kernel,
        grid=(depth, d3 // d),
        in_specs=[
            pl.BlockSpec((16, d), lambda l, j: (0, 0)),
            pl.BlockSpec((None, d, d), lambda l, j: (l, 0, j)),
            pl.BlockSpec((None, 1, d), lambda l, j: (l, 0, j)),
        ],
        out_specs=pl.BlockSpec((None, 16, d), lambda l, j: (l, 0, j)),
        out_shape=jax.ShapeDtypeStruct((depth, 16, d3), F32),
        compiler_params=pltpu.CompilerParams(vmem_limit_bytes=VMEM_LIMIT),
        name="adaln_tables",
    )(cond, w_ada, b_ada.reshape(depth, 1, d3))


def _attn_in_kernel(h_ref, mod_ref, w_ref, cos_ref, sa_ref, sb_ref, q_ref, k_ref, v_ref, g_ref):
    d = h_ref.shape[-1]
    kvw = ATTN_KV_HEADS * HEAD_DIM
    ub = _norm_modulate(h_ref[...], mod_ref).astype(BF16)
    cos = cos_ref[...]
    sa = sa_ref[...]
    sb = sb_ref[...]

    def rope(t):
        return t * cos + pltpu.roll(t, LANES - 16, 1) * sa + pltpu.roll(t, 16, 1) * sb

    left = lax.broadcasted_iota(jnp.int32, (h_ref.shape[0], LANES), 1) < HEAD_DIM

    def store_doubled(ref, j, t):
        sw = pltpu.roll(t, HEAD_DIM, 1)
        ref[:, (2 * j) * LANES:(2 * j + 1) * LANES] = jnp.where(left, t, sw).astype(BF16)
        ref[:, (2 * j + 1) * LANES:(2 * j + 2) * LANES] = jnp.where(left, sw, t).astype(BF16)

    q = jnp.dot(ub, w_ref[:, 0:d], preferred_element_type=F32)
    scale = HEAD_DIM ** -0.5 * math.log2(math.e)
    for j in range(d // LANES):
        sl = slice(j * LANES, (j + 1) * LANES)
        q_ref[:, sl] = (rope(q[:, sl]) * scale).astype(BF16)
    k = jnp.dot(ub, w_ref[:, d:d + kvw], preferred_element_type=F32)
    v = jnp.dot(ub, w_ref[:, d + kvw:d + 2 * kvw], preferred_element_type=F32)
    for j in range(kvw // LANES):
        sl = slice(j * LANES, (j + 1) * LANES)
        store_doubled(k_ref, j, rope(k[:, sl]))
        store_doubled(v_ref, j, v[:, sl])
    g = jnp.dot(ub, w_ref[:, d + 2 * kvw:], preferred_element_type=F32)
    g_ref[...] = _silu(g).astype(BF16)


def _attn_in(hs, mods, layer, w_in, cos, sa, sb):
    bsz, t, d = hs.shape
    kvw = ATTN_KV_HEADS * HEAD_DIM
    n_in = w_in.shape[1]
    ctx_tiles = CTX_LEN // ROW_TILE
    row_spec = lambda width: pl.BlockSpec((None, ROW_TILE, width), lambda b, i: (b, i, 0))
    tab_spec = pl.BlockSpec((ROW_TILE, LANES), lambda b, i: (i, 0))
    return pl.pallas_call(
        _attn_in_kernel,
        grid=(bsz, t // ROW_TILE),
        in_specs=[
            row_spec(d),
            pl.BlockSpec((None, None, 3, d), lambda b, i: (layer, _mod_row(i < ctx_tiles, b), 0, 0)),
            pl.BlockSpec((d, n_in), lambda b, i: (0, 0)),
            tab_spec, tab_spec, tab_spec,
        ],
        out_specs=[row_spec(d), row_spec(2 * kvw), row_spec(2 * kvw), row_spec(d)],
        out_shape=[
            jax.ShapeDtypeStruct((bsz, t, d), BF16),
            jax.ShapeDtypeStruct((bsz, t, 2 * kvw), BF16),
            jax.ShapeDtypeStruct((bsz, t, 2 * kvw), BF16),
            jax.ShapeDtypeStruct((bsz, t, d), BF16),
        ],
        compiler_params=pltpu.CompilerParams(
            dimension_semantics=("parallel", "parallel"), vmem_limit_bytes=VMEM_LIMIT),
        name="attn_in",
    )(hs, mods, w_in, cos, sa, sb)


def _attn_core_kernel(n_lat_blocks, sink_ref, q_ref, kp_ref, kc_ref, kn_ref, kx_ref,
                      vp_ref, vc_ref, vn_ref, vx_ref, g_ref, h_ref, mod_ref, wo_ref, o_ref,
                      kcat, vdiag, ocat):
    blk = pl.program_id(1)
    ctx_blocks = CTX_LEN // ATTN_BLOCK
    j = blk - ctx_blocks
    nb = ATTN_BLOCK
    n_keys = 3 * nb + CTX_LEN
    pairs = ATTN_GROUP // 2
    log2e = math.log2(math.e)
    nt_dims = (((1,), (1,)), ((), ()))

    def left(rows):
        return lax.broadcasted_iota(jnp.int32, (rows, LANES), 1) < HEAD_DIM

    def split_heads(t):
        zero = jnp.zeros_like(t)
        return jnp.where(left(t.shape[0]), t, zero), jnp.where(left(t.shape[0]), zero, t)

    def stacked_queries(kh):
        rows = []
        for pr in range(pairs):
            rows.extend(split_heads(q_ref[:, (kh * pairs + pr) * LANES:(kh * pairs + pr + 1) * LANES]))
        return jnp.concatenate(rows, axis=0)

    def softmax_pv(kh, s, pv):
        sink = jnp.concatenate(
            [jnp.full((nb, 1), sink_ref[kh * ATTN_GROUP + r] * log2e, F32) for r in range(ATTN_GROUP)],
            axis=0)
        m = jnp.maximum(jnp.max(s, axis=-1, keepdims=True), sink)
        p = jnp.exp2(s - m)
        den = jnp.sum(p, axis=-1, keepdims=True) + jnp.exp2(sink - m)
        r = 1.0 / den
        pb = p.astype(BF16)
        for pr in range(pairs):
            lo, mid, hi = 2 * pr * nb, (2 * pr + 1) * nb, (2 * pr + 2) * nb
            o = pv(pb[lo:mid], pb[mid:hi])
            rr = jnp.where(left(nb), r[lo:mid], r[mid:hi])
            ocat[:, (kh * pairs + pr) * LANES:(kh * pairs + pr + 1) * LANES] = o * rr

    @pl.when(blk >= ctx_blocks)
    def _():
        for i, (k_ref, v_ref) in enumerate(
                ((kp_ref, vp_ref), (kc_ref, vc_ref), (kn_ref, vn_ref), (kx_ref, vx_ref))):
            r0, r1 = i * nb, i * nb + k_ref.shape[0]
            kcat[r0:r1, :] = k_ref[...]
            for kh in range(ATTN_KV_HEADS):
                top, bot = split_heads(v_ref[:, kh * LANES:(kh + 1) * LANES])
                vdiag[kh, r0:r1, :] = top
                vdiag[kh, n_keys + r0:n_keys + r1, :] = bot
        row = lax.broadcasted_iota(jnp.int32, (nb, nb), 0)
        col = lax.broadcasted_iota(jnp.int32, (nb, nb), 1)
        bias_prev = jnp.where((col >= row) & (j > 0), 0.0, MASKED).astype(F32)
        bias_next = jnp.where((col <= row) & (j < n_lat_blocks - 1), 0.0, MASKED).astype(F32)
        bias_prev = jnp.concatenate([bias_prev] * ATTN_GROUP, axis=0)
        bias_next = jnp.concatenate([bias_next] * ATTN_GROUP, axis=0)
        for kh in range(ATTN_KV_HEADS):
            s = lax.dot_general(stacked_queries(kh), kcat[:, kh * LANES:(kh + 1) * LANES], nt_dims,
                                preferred_element_type=F32)
            s = jnp.concatenate([s[:, 0:nb] + bias_prev, s[:, nb:2 * nb],
                                 s[:, 2 * nb:3 * nb] + bias_next, s[:, 3 * nb:]], axis=1)
            softmax_pv(kh, s, lambda pa, pc, kh=kh: jnp.dot(
                jnp.concatenate([pa, pc], axis=1), vdiag[kh], preferred_element_type=F32))

    @pl.when(blk < ctx_blocks)
    def _():
        for kh in range(ATTN_KV_HEADS):
            s = lax.dot_general(stacked_queries(kh), kx_ref[:, kh * LANES:(kh + 1) * LANES], nt_dims,
                                preferred_element_type=F32)
            top, bot = split_heads(vx_ref[:, kh * LANES:(kh + 1) * LANES])
            softmax_pv(kh, s, lambda pa, pc, top=top, bot=bot: (
                jnp.dot(pa, top, preferred_element_type=F32)
                + jnp.dot(pc, bot, preferred_element_type=F32)))

    og = (ocat[...] * g_ref[...].astype(F32)).astype(BF16)
    y = jnp.dot(og, wo_ref[...], preferred_element_type=F32)
    o_ref[...] = h_ref[...] + mod_ref[2:3, :] * y


def _attn_core(hs, mods, layer, q, k, v, g, sink, w_out):
    bsz, t, d = hs.shape
    kvw = ATTN_KV_HEADS * LANES
    nb = ATTN_BLOCK
    ctx_blocks = CTX_LEN // nb
    n_blocks = t // nb
    n_lat = n_blocks - ctx_blocks
    blk_spec = lambda width, imap: pl.BlockSpec((None, nb, width), imap)
    own = lambda b, i: (b, i, 0)
    prev = lambda b, i: (b, jnp.maximum(i - 1, ctx_blocks), 0)
    nxt = lambda b, i: (b, jnp.minimum(jnp.maximum(i + 1, ctx_blocks), n_blocks - 1), 0)
    ctx_spec = pl.BlockSpec((None, CTX_LEN, kvw), lambda b, i: (b, 0, 0))
    kv_specs = [blk_spec(kvw, prev), blk_spec(kvw, own), blk_spec(kvw, nxt), ctx_spec]
    return pl.pallas_call(
        functools.partial(_attn_core_kernel, n_lat),
        grid=(bsz, n_blocks),
        in_specs=[
            pl.BlockSpec(memory_space=pltpu.SMEM),
            blk_spec(d, own),
            *kv_specs, *kv_specs,
            blk_spec(d, own),
            blk_spec(d, own),
            pl.BlockSpec((None, None, 3, d), lambda b, i: (layer, _mod_row(i < ctx_blocks, b), 0, 0)),
            pl.BlockSpec((d, d), lambda b, i: (0, 0)),
        ],
        out_specs=blk_spec(d, own),
        out_shape=jax.ShapeDtypeStruct((bsz, t, d), F32),
        scratch_shapes=[
            pltpu.VMEM((3 * nb + CTX_LEN, kvw), BF16),
            pltpu.VMEM((ATTN_KV_HEADS, 2 * (3 * nb + CTX_LEN), LANES), BF16),
            pltpu.VMEM((nb, d), F32),
        ],
        compiler_params=pltpu.CompilerParams(
            dimension_semantics=("parallel", "parallel"), vmem_limit_bytes=VMEM_LIMIT),
        name="attn_core",
    )(sink, q, k, k, k, k, v, v, v, v, g, hs, mods, w_out)


def _ssd_in_kernel(h_ref, hp_ref, hn_ref, mod_ref, wz_ref, wx_ref, wdt_ref, cw_ref, cb_ref, dtb_ref,
                   z_ref, xs_ref, bm_ref, cm_ref, dt_ref):
    blk = pl.program_id(1)
    n_tiles = pl.num_programs(1)
    ctx_tiles = CTX_LEN // ROW_TILE
    halo = SUBLANES
    tm = ROW_TILE
    width = xs_ref.shape[-1]
    gs_width = bm_ref.shape[-1]

    prev_ok = (blk > ctx_tiles) | ((blk > 0) & (blk < ctx_tiles))
    next_ok = ((blk >= ctx_tiles) & (blk < n_tiles - 1)) | (blk < ctx_tiles - 1)
    u_main = _norm_modulate(h_ref[...], mod_ref)
    u_all = jnp.concatenate(
        [_norm_modulate(hp_ref[...], mod_ref) * prev_ok.astype(F32), u_main,
         _norm_modulate(hn_ref[...], mod_ref) * next_ok.astype(F32)], axis=0).astype(BF16)
    ub = u_main.astype(BF16)

    z_ref[...] = jnp.dot(ub, wz_ref[...], preferred_element_type=F32).astype(BF16)
    dt_raw = jnp.dot(ub, wdt_ref[...], preferred_element_type=F32) + dtb_ref[...]
    dt_ref[...] = jnp.maximum(dt_raw, 0.0) + jnp.log1p(jnp.exp(-jnp.abs(dt_raw)))

    pad = SSD_CONV // 2
    n_rows = tm + 2 * halo
    sub = lax.broadcasted_iota(jnp.int32, (1, SUBLANES, CONV_COL_TILE), 1)
    for c in range(wx_ref.shape[-1] // CONV_COL_TILE):
        cs = slice(c * CONV_COL_TILE, (c + 1) * CONV_COL_TILE)
        pre = jnp.dot(u_all, wx_ref[:, cs], preferred_element_type=F32)
        tiles = pre.reshape(n_rows // SUBLANES, SUBLANES, CONV_COL_TILE)
        n_out = tm // SUBLANES
        acc = cb_ref[:, cs] + cw_ref[pad:pad + 1, cs] * tiles[1:1 + n_out]
        for kk in range(SSD_CONV):
            off = kk - pad
            if off == 0:
                continue
            rot = pltpu.roll(tiles, (-off) % SUBLANES, 1)
            same = (sub + off < SUBLANES) if off > 0 else (sub + off >= 0)
            nbr = 1 + (1 if off > 0 else -1)
            acc = acc + cw_ref[kk:kk + 1, cs] * jnp.where(same, rot[1:1 + n_out], rot[nbr:nbr + n_out])
        out = _silu(acc).reshape(tm, CONV_COL_TILE).astype(BF16)
        lo = c * CONV_COL_TILE
        if lo < width:
            xs_ref[:, lo:lo + CONV_COL_TILE] = out
        elif lo < width + gs_width:
            bm_ref[:, lo - width:lo - width + CONV_COL_TILE] = out
        else:
            cm_ref[:, lo - width - gs_width:lo - width - gs_width + CONV_COL_TILE] = out


def _ssd_in(hs, mods, layer, w_z, w_xbc, w_dt, conv_w, conv_b, dt_bias):
    bsz, t, d = hs.shape
    width = w_z.shape[1]
    gs_width = SSD_GROUPS * SSD_STATE
    n_tiles = t // ROW_TILE
    ctx_tiles = CTX_LEN // ROW_TILE
    per_tile = ROW_TILE // SUBLANES
    n_halo_blocks = t // SUBLANES
    row_spec = lambda w_: pl.BlockSpec((None, ROW_TILE, w_), lambda b, i: (b, i, 0))
    full = lambda a: pl.BlockSpec(a.shape, lambda b, i: (0,) * a.ndim)
    return pl.pallas_call(
        _ssd_in_kernel,
        grid=(bsz, n_tiles),
        in_specs=[
            row_spec(d),
            pl.BlockSpec((None, SUBLANES, d), lambda b, i: (b, jnp.maximum(i * per_tile - 1, 0), 0)),
            pl.BlockSpec((None, SUBLANES, d),
                         lambda b, i: (b, jnp.minimum((i + 1) * per_tile, n_halo_blocks - 1), 0)),
            pl.BlockSpec((None, None, 3, d), lambda b, i: (layer, _mod_row(i < ctx_tiles, b), 0, 0)),
            full(w_z), full(w_xbc), full(w_dt), full(conv_w), full(conv_b), full(dt_bias),
        ],
        out_specs=[row_spec(width), row_spec(width), row_spec(gs_width), row_spec(gs_width),
                   row_spec(LANES)],
        out_shape=[
            jax.ShapeDtypeStruct((bsz, t, width), BF16),
            jax.ShapeDtypeStruct((bsz, t, width), BF16),
            jax.ShapeDtypeStruct((bsz, t, gs_width), BF16),
            jax.ShapeDtypeStruct((bsz, t, gs_width), BF16),
            jax.ShapeDtypeStruct((bsz, t, LANES), F32),
        ],
        compiler_params=pltpu.CompilerParams(
            dimension_semantics=("parallel", "parallel"), vmem_limit_bytes=VMEM_LIMIT),
        name="ssd_in",
    )(hs, hs, hs, mods, w_z, w_xbc, w_dt, conv_w, conv_b, dt_bias)


def _ssd_chunk(direction, xs_ref, bm_ref, cm_ref, dt_ref, alog_ref, st_ref, emit, skip_ref=None):
    q = SSD_CHUNK
    log2e = math.log2(math.e)
    dt = dt_ref[...]
    da = dt * (-jnp.exp(alog_ref[...]))
    row = lax.broadcasted_iota(jnp.int32, (q, q), 0)
    col = lax.broadcasted_iota(jnp.int32, (q, q), 1)
    seen = (row >= col) if direction == 0 else (col >= row)
    a = jnp.dot(seen.astype(F32), da, preferred_element_type=F32,
                precision=lax.Precision.HIGHEST)
    last = q - 1 if direction == 0 else 0
    a_tot = a[last:last + 1, :]
    w_end = jnp.exp(a_tot - a) * dt
    dec = jnp.exp(a_tot)
    a2 = a * log2e
    src2_t = ((a - jnp.log(dt)) * log2e).T
    left = col < HEAD_DIM

    for g in range(SSD_GROUPS):
        gsl = slice(g * SSD_STATE, (g + 1) * SSD_STATE)
        cmg = cm_ref[:, gsl]
        bmg = bm_ref[:, gsl]
        cb = lax.dot_general(cmg, bmg, (((1,), (1,)), ((), ())), preferred_element_type=F32)
        h_in = st_ref[g]
        y_off = jnp.dot(cmg, h_in.astype(BF16), preferred_element_type=F32)
        ys, xws, decs = [], [], []
        for pr in range(SSD_HEADS_PER_GROUP // 2):
            pair = g * (SSD_HEADS_PER_GROUP // 2) + pr
            c0 = direction * SSD_HEADS + 2 * pair
            psl = slice(pair * LANES, (pair + 1) * LANES)
            xp = xs_ref[:, psl]
            ws, eas = [], []
            for c in (c0, c0 + 1):
                a_col = jnp.broadcast_to(a2[:, c:c + 1], (q, q))
                lmat = jnp.exp2(jnp.where(seen, a_col - src2_t[c:c + 1, :], -jnp.inf))
                ws.append((cb * lmat).astype(BF16))
                eas.append(jnp.exp2(a_col))
            zero = jnp.zeros_like(xp)
            x_diag = jnp.concatenate([jnp.where(left, xp, zero), jnp.where(left, zero, xp)], axis=0)
            y_diag = jnp.dot(jnp.concatenate(ws, axis=1), x_diag, preferred_element_type=F32)
            y = y_diag + y_off[:, pr * LANES:(pr + 1) * LANES] * jnp.where(left, eas[0], eas[1])
            xf = xp.astype(F32)
            if skip_ref is not None:
                y = y + skip_ref[:, psl] * xf
            ys.append(y)
            idx = jnp.where(left, c0, c0 + 1)
            xws.append((xf * jnp.take_along_axis(w_end, idx, axis=1)).astype(BF16))
            decs.append(jnp.where(left[0:1, :], jnp.broadcast_to(dec[:, c0:c0 + 1], (1, LANES)),
                                  jnp.broadcast_to(dec[:, c0 + 1:c0 + 2], (1, LANES))))
        emit(g, jnp.concatenate(ys, axis=1))
        new = lax.dot_general(bmg, jnp.concatenate(xws, axis=1), (((0,), (0,)), ((), ())),
                              preferred_element_type=F32)
        st_ref[g] = h_in * jnp.concatenate(decs, axis=1) + new


def _ssd_fwd_kernel(xs_ref, bm_ref, cm_ref, dt_ref, alog_ref, dsk_ref, y_ref, st_ref):
    @pl.when(pl.program_id(1) == 0)
    def _():
        st_ref[...] = jnp.zeros_like(st_ref)

    def emit(g, y):
        y_ref[:, g * SSD_GROUP_WIDTH:(g + 1) * SSD_GROUP_WIDTH] = y

    _ssd_chunk(0, xs_ref, bm_ref, cm_ref, dt_ref, alog_ref, st_ref, emit, skip_ref=dsk_ref)


def _ssd_bwd_kernel(xs_ref, bm_ref, cm_ref, dt_ref, alog_ref, yf_ref, z_ref, h_ref, mod_ref, nw_ref,
                    wo_ref, o_ref, st_ref, u_ref):
    @pl.when(pl.program_id(1) == 0)
    def _():
        st_ref[...] = jnp.zeros_like(st_ref)

    def emit(g, y):
        sl = slice(g * SSD_GROUP_WIDTH, (g + 1) * SSD_GROUP_WIDTH)
        u = (y + yf_ref[:, sl]) * _silu(z_ref[:, sl].astype(F32))
        ms = jnp.mean(u * u, axis=-1, keepdims=True)
        u_ref[:, sl] = (u * lax.rsqrt(ms + EPS) * nw_ref[:, sl]).astype(BF16)

    _ssd_chunk(1, xs_ref, bm_ref, cm_ref, dt_ref, alog_ref, st_ref, emit)
    y = jnp.dot(u_ref[...], wo_ref[...], preferred_element_type=F32)
    o_ref[...] = h_ref[...] + mod_ref[2:3, :] * y


def _ssd_scans(hs, mods, layer, z, xs, bm, cm, dt, a_log, d_skip, norm_w, w_out):
    bsz, t, d = hs.shape
    width = xs.shape[-1]
    gs_width = bm.shape[-1]
    q = SSD_CHUNK
    n_chunks = t // q
    ctx_chunks = CTX_LEN // q
    fwd = lambda b, i: (b, i, 0)
    bwd_chunk = lambda i: jnp.where(i < ctx_chunks, ctx_chunks - 1 - i, n_chunks - 1 + ctx_chunks - i)
    bwd = lambda b, i: (b, bwd_chunk(i), 0)
    const = lambda a: pl.BlockSpec(a.shape, lambda b, i: (0,) * a.ndim)
    state = pltpu.VMEM((SSD_GROUPS, SSD_STATE, SSD_GROUP_WIDTH), F32)
    chunk_specs = lambda imap: [
        pl.BlockSpec((None, q, width), imap), pl.BlockSpec((None, q, gs_width), imap),
        pl.BlockSpec((None, q, gs_width), imap), pl.BlockSpec((None, q, LANES), imap)]

    y_f = pl.pallas_call(
        _ssd_fwd_kernel,
        grid=(bsz, n_chunks),
        in_specs=[*chunk_specs(fwd), const(a_log), const(d_skip)],
        out_specs=pl.BlockSpec((None, q, width), fwd),
        out_shape=jax.ShapeDtypeStruct((bsz, t, width), F32),
        scratch_shapes=[state],
        compiler_params=pltpu.CompilerParams(
            dimension_semantics=("parallel", "arbitrary"), vmem_limit_bytes=VMEM_LIMIT),
        name="ssd_fwd",
    )(xs, bm, cm, dt, a_log, d_skip)

    return pl.pallas_call(
        _ssd_bwd_kernel,
        grid=(bsz, n_chunks),
        in_specs=[
            *chunk_specs(bwd), const(a_log),
            pl.BlockSpec((None, q, width), bwd),
            pl.BlockSpec((None, q, width), bwd),
            pl.BlockSpec((None, q, d), bwd),
            pl.BlockSpec((None, None, 3, d),
                         lambda b, i: (layer, _mod_row(bwd_chunk(i) < ctx_chunks, b), 0, 0)),
            const(norm_w), const(w_out),
        ],
        out_specs=pl.BlockSpec((None, q, d), bwd),
        out_shape=jax.ShapeDtypeStruct((bsz, t, d), F32),
        scratch_shapes=[state, pltpu.VMEM((q, width), BF16)],
        compiler_params=pltpu.CompilerParams(
            dimension_semantics=("parallel", "arbitrary"), vmem_limit_bytes=VMEM_LIMIT),
        name="ssd_bwd",
    )(xs, bm, cm, dt, a_log, y_f, z, hs, mods, norm_w, w_out)


def _final_norm_kernel(h_ref, w_ref, o_ref):
    x = h_ref[...]
    ms = jnp.mean(x * x, axis=-1, keepdims=True)
    o_ref[...] = x * lax.rsqrt(ms + EPS) * w_ref[...]


def _final_norm(hs, w):
    bsz, t, d = hs.shape
    ctx_tiles = CTX_LEN // ROW_TILE
    n_lat = t - CTX_LEN
    return pl.pallas_call(
        _final_norm_kernel,
        grid=(bsz, n_lat // ROW_TILE),
        in_specs=[pl.BlockSpec((None, ROW_TILE, d), lambda b, i: (b, i + ctx_tiles, 0)),
                  pl.BlockSpec((1, d), lambda b, i: (0, 0))],
        out_specs=pl.BlockSpec((None, ROW_TILE, d), lambda b, i: (b, i, 0)),
        out_shape=jax.ShapeDtypeStruct((bsz, n_lat, d), F32),
        compiler_params=pltpu.CompilerParams(
            dimension_semantics=("parallel", "parallel"), vmem_limit_bytes=VMEM_LIMIT),
        name="final_norm",
    )(hs, w)


def _rope_tables(seq_len):
    quarter = HEAD_DIM // 4
    pos = jnp.arange(seq_len, dtype=jnp.int32)
    row = (pos // GRID_W).astype(F32)
    col = (pos % GRID_W).astype(F32)
    inv_freq = ROPE_BASE ** (-jnp.arange(0, 2 * quarter, 2, dtype=F32) / (2 * quarter))
    lane = jnp.arange(LANES)
    in_head = lane % HEAD_DIM
    p = jnp.where((in_head < HEAD_DIM // 2)[None, :], row[:, None], col[:, None])
    ang = p * inv_freq[lane % quarter][None, :]
    first = ((lane % (2 * quarter)) < quarter)[None, :]
    cos = jnp.cos(ang)
    sin = jnp.sin(ang)
    sa = jnp.where(first, -sin, 0.0)
    sb = jnp.where(first, 0.0, sin)
    ident = jnp.ones((CTX_LEN, LANES), F32)
    zeros = jnp.zeros((CTX_LEN, LANES), F32)
    return (jnp.concatenate([ident, cos], 0), jnp.concatenate([zeros, sa], 0),
            jnp.concatenate([zeros, sb], 0))


def kernel(x, c, ctx, c_ctx, w_ada, b_ada, attn_w_in, attn_sink, attn_w_out, ssd_w_in, ssd_conv_w,
           ssd_conv_b, ssd_dt_bias, ssd_a_log, ssd_d, ssd_norm_w, ssd_w_out, final_norm_w):
    bsz, seq_len, d = x.shape
    depth = w_ada.shape[0]
    assert ctx.shape[1] == CTX_LEN and CTX_LEN % ROW_TILE == 0 and seq_len % ROW_TILE == 0
    assert bsz <= 8

    cond = jnp.zeros((16, d), F32).at[:bsz].set(c).at[8].set(c_ctx)
    mods = _modulation_tables(cond, w_ada, b_ada).reshape(depth, 16, 3, d)
    cos, sa, sb = _rope_tables(seq_len)
    hs = jnp.concatenate([ctx, x], axis=1)

    ssd_width = ssd_w_out.shape[1]
    conv_dim = ssd_conv_w.shape[-1]
    pad_lanes = lambda a: jnp.pad(a, ((0, 0), (0, LANES - a.shape[-1])))

    for i in range(depth):
        j = i // 2
        if i % 2 == 0:
            q, k, v, g = _attn_in(hs, mods, i, attn_w_in[j].astype(BF16), cos, sa, sb)
            hs = _attn_core(hs, mods, i, q, k, v, g, attn_sink[j], attn_w_out[j].astype(BF16))
        else:
            w_in = ssd_w_in[j]
            w_z = w_in[:, :ssd_width].astype(BF16)
            w_xbc = w_in[:, ssd_width:ssd_width + conv_dim].astype(BF16)
            w_dt = pad_lanes(w_in[:, ssd_width + conv_dim:]).astype(BF16)
            z, xs, bm, cm, dt = _ssd_in(
                hs, mods, i, w_z, w_xbc, w_dt, ssd_conv_w[j], ssd_conv_b[j][None, :],
                pad_lanes(ssd_dt_bias[j].reshape(1, -1)))
            hs = _ssd_scans(
                hs, mods, i, z, xs, bm, cm, dt, pad_lanes(ssd_a_log[j].reshape(1, -1)),
                jnp.repeat(ssd_d[j], HEAD_DIM)[None, :], ssd_norm_w[j][None, :],
                ssd_w_out[j].astype(BF16))
    return _final_norm(hs, final_norm_w[None, :])
```

```python
import functools
import math

import jax
import jax.numpy as jnp
from jax import lax
from jax.experimental import pallas as pl
from jax.experimental.pallas import tpu as pltpu

F32 = jnp.float32
BF16 = jnp.bfloat16

EPS = 1e-6
CTX_LEN = 256
GRID_W = 64
ROPE_BASE = 10000.0

HEAD_DIM = 64
ATTN_HEADS = 16
ATTN_KV_HEADS = 4
ATTN_GROUP = ATTN_HEADS // ATTN_KV_HEADS
ATTN_BLOCK = 128
LANES = 128
SUBLANES = 8

SSD_HEADS = 32
SSD_GROUPS = 8
SSD_HEADS_PER_GROUP = SSD_HEADS // SSD_GROUPS
SSD_STATE = 128
SSD_CONV = 5
SSD_CHUNK = 128
SSD_GROUP_WIDTH = SSD_HEADS_PER_GROUP * HEAD_DIM

ROW_TILE = 256
CONV_COL_TILE = 512
MASKED = -1e30
VMEM_LIMIT = 56 * 1024 * 1024


def _silu(x):
    return x / (1.0 + jnp.exp2(x * -math.log2(math.e)))


def _norm_modulate(x, mod_ref):
    ms = jnp.mean(x * x, axis=-1, keepdims=True)
    return x * lax.rsqrt(ms + EPS) * (1.0 + mod_ref[1:2, :]) + mod_ref[0:1, :]


def _mod_row(is_ctx, b):
    return jnp.where(is_ctx, 8, b)


def _mod_kernel(cond_ref, w_ref, b_ref, o_ref):
    cnd = cond_ref[...]
    o_ref[...] = jnp.dot(_silu(cnd), w_ref[...], preferred_element_type=F32,
                         precision=lax.Precision.HIGHEST) + b_ref[...]


def _modulation_tables(cond, w_ada, b_ada):
    depth, d, d3 = w_ada.shape
    return pl.pallas_call(
        _mod_kernel,
        grid=(depth, d3 // d),
        in_specs=[
            pl.BlockSpec((16, d), lambda l, j: (0, 0)),
            pl.BlockSpec((None, d, d), lambda l, j: (l, 0, j)),
            pl.BlockSpec((None, 1, d), lambda l, j: (l, 0, j)),
        ],
        out_specs=pl.BlockSpec((None, 16, d), lambda l, j: (l, 0, j)),
        out_shape=jax.ShapeDtypeStruct((depth, 16, d3), F32),
        compiler_params=pltpu.CompilerParams(vmem_limit_bytes=VMEM_LIMIT),
        name="adaln_tables",
    )(cond, w_ada, b_ada.reshape(depth, 1, d3))


def _attn_in_kernel(h_ref, mod_ref, w_ref, cos_ref, sa_ref, sb_ref, q_ref, k_ref, v_ref, g_ref):
    d = h_ref.shape[-1]
    kvw = ATTN_KV_HEADS * HEAD_DIM
    ub = _norm_modulate(h_ref[...], mod_ref).astype(BF16)
    cos = cos_ref[...]
    sa = sa_ref[...]
    sb = sb_ref[...]

    def rope(t):
        return t * cos + pltpu.roll(t, LANES - 16, 1) * sa + pltpu.roll(t, 16, 1) * sb

    left = lax.broadcasted_iota(jnp.int32, (h_ref.shape[0], LANES), 1) < HEAD_DIM

    def store_doubled(ref, j, t):
        sw = pltpu.roll(t, HEAD_DIM, 1)
        ref[:, (2 * j) * LANES:(2 * j + 1) * LANES] = jnp.where(left, t, sw).astype(BF16)
        ref[:, (2 * j + 1) * LANES:(2 * j + 2) * LANES] = jnp.where(left, sw, t).astype(BF16)

    q = jnp.dot(ub, w_ref[:, 0:d], preferred_element_type=F32)
    scale = HEAD_DIM ** -0.5 * math.log2(math.e)
    for j in range(d // LANES):
        sl = slice(j * LANES, (j + 1) * LANES)
        q_ref[:, sl] = (rope(q[:, sl]) * scale).astype(BF16)
    k = jnp.dot(ub, w_ref[:, d:d + kvw], preferred_element_type=F32)
    v = jnp.dot(ub, w_ref[:, d + kvw:d + 2 * kvw], preferred_element_type=F32)
    for j in range(kvw // LANES):
        sl = slice(j * LANES, (j + 1) * LANES)
        store_doubled(k_ref, j, rope(k[:, sl]))
        store_doubled(v_ref, j, v[:, sl])
    g = jnp.dot(ub, w_ref[:, d + 2 * kvw:], preferred_element_type=F32)
    g_ref[...] = _silu(g).astype(BF16)


def _attn_in(hs, mods, layer, w_in, cos, sa, sb):
    bsz, t, d = hs.shape
    kvw = ATTN_KV_HEADS * HEAD_DIM
    n_in = w_in.shape[1]
    ctx_tiles = CTX_LEN // ROW_TILE
    row_spec = lambda width: pl.BlockSpec((None, ROW_TILE, width), lambda b, i: (b, i, 0))
    tab_spec = pl.BlockSpec((ROW_TILE, LANES), lambda b, i: (i, 0))
    return pl.pallas_call(
        _attn_in_kernel,
        grid=(bsz, t // ROW_TILE),
        in_specs=[
            row_spec(d),
            pl.BlockSpec((None, None, 3, d), lambda b, i: (layer, _mod_row(i < ctx_tiles, b), 0, 0)),
            pl.BlockSpec((d, n_in), lambda b, i: (0, 0)),
            tab_spec, tab_spec, tab_spec,
        ],
        out_specs=[row_spec(d), row_spec(2 * kvw), row_spec(2 * kvw), row_spec(d)],
        out_shape=[
            jax.ShapeDtypeStruct((bsz, t, d), BF16),
            jax.ShapeDtypeStruct((bsz, t, 2 * kvw), BF16),
            jax.ShapeDtypeStruct((bsz, t, 2 * kvw), BF16),
            jax.ShapeDtypeStruct((bsz, t, d), BF16),
        ],
        compiler_params=pltpu.CompilerParams(
            dimension_semantics=("parallel", "parallel"), vmem_limit_bytes=VMEM_LIMIT),
        name="attn_in",
    )(hs, mods, w_in, cos, sa, sb)


def _attn_core_kernel(n_lat_blocks, sink_ref, q_ref, kp_ref, kc_ref, kn_ref, kx_ref,
                      vp_ref, vc_ref, vn_ref, vx_ref, g_ref, h_ref, mod_ref, wo_ref, o_ref,
                      kcat, vdiag, ocat):
    blk = pl.program_id(1)
    ctx_blocks = CTX_LEN // ATTN_BLOCK
    j = blk - ctx_blocks
    nb = ATTN_BLOCK
    n_keys = 3 * nb + CTX_LEN
    pairs = ATTN_GROUP // 2
    log2e = math.log2(math.e)
    nt_dims = (((1,), (1,)), ((), ()))

    def left(rows):
        return lax.broadcasted_iota(jnp.int32, (rows, LANES), 1) < HEAD_DIM

    def split_heads(t):
        zero = jnp.zeros_like(t)
        return jnp.where(left(t.shape[0]), t, zero), jnp.where(left(t.shape[0]), zero, t)

    def stacked_queries(kh):
        rows = []
        for pr in range(pairs):
            rows.extend(split_heads(q_ref[:, (kh * pairs + pr) * LANES:(kh * pairs + pr + 1) * LANES]))
        return jnp.concatenate(rows, axis=0)

    def softmax(kh, s):
        sink = jnp.concatenate(
            [jnp.full((nb, 1), sink_ref[kh * ATTN_GROUP + r] * log2e, F32) for r in range(ATTN_GROUP)],
            axis=0)
        m = jnp.maximum(jnp.max(s, axis=-1, keepdims=True), sink)
        p = jnp.exp2(s - m)
        den = jnp.sum(p, axis=-1, keepdims=True) + jnp.exp2(sink - m)
        return p.astype(BF16), 1.0 / den

    def weighted_values(kh, pb, r, n_keys):
        for pr in range(pairs):
            lo, mid, hi = 2 * pr * nb, (2 * pr + 1) * nb, (2 * pr + 2) * nb
            o = jnp.dot(jnp.concatenate([pb[lo:mid], pb[mid:hi]], axis=1), vdiag[kh, 0:2 * n_keys, :],
                        preferred_element_type=F32)
            rr = jnp.where(left(nb), r[lo:mid], r[mid:hi])
            ocat[:, (kh * pairs + pr) * LANES:(kh * pairs + pr + 1) * LANES] = o * rr

    def attend(key_blocks, bias):
        n_keys = sum(k_ref.shape[0] for k_ref, _ in key_blocks)
        r0 = 0
        for k_ref, v_ref in key_blocks:
            r1 = r0 + k_ref.shape[0]
            kcat[r0:r1, :] = k_ref[...]
            for kh in range(ATTN_KV_HEADS):
                top, bot = split_heads(v_ref[:, kh * LANES:(kh + 1) * LANES])
                vdiag[kh, r0:r1, :] = top
                vdiag[kh, n_keys + r0:n_keys + r1, :] = bot
            r0 = r1

        def logits(kh):
            s = lax.dot_general(stacked_queries(kh), kcat[0:n_keys, kh * LANES:(kh + 1) * LANES],
                                nt_dims, preferred_element_type=F32)
            if not bias:
                return s
            return jnp.concatenate(
                [s[:, i * nb:(i + 1) * nb] + bias[i] if i in bias else s[:, i * nb:(i + 1) * nb]
                 for i in range(n_keys // nb)], axis=1)

        s_next = logits(0)
        pending = None
        for kh in range(ATTN_KV_HEADS):
            s = s_next
            if kh + 1 < ATTN_KV_HEADS:
                s_next = logits(kh + 1)
            pb, r = softmax(kh, s)
            if pending is not None:
                weighted_values(*pending, n_keys)
            pending = (kh, pb, r)
        weighted_values(*pending, n_keys)

    @pl.when(blk >= ctx_blocks)
    def _():
        row = lax.broadcasted_iota(jnp.int32, (nb, nb), 0)
        col = lax.broadcasted_iota(jnp.int32, (nb, nb), 1)
        bias_prev = jnp.where((col >= row) & (j > 0), 0.0, MASKED).astype(F32)
        bias_next = jnp.where((col <= row) & (j < n_lat_blocks - 1), 0.0, MASKED).astype(F32)
        attend(((kp_ref, vp_ref), (kc_ref, vc_ref), (kn_ref, vn_ref), (kx_ref, vx_ref)),
               {0: jnp.concatenate([bias_prev] * ATTN_GROUP, axis=0),
                2: jnp.concatenate([bias_next] * ATTN_GROUP, axis=0)})

    @pl.when(blk < ctx_blocks)
    def _():
        attend(((kx_ref, vx_ref),), {})

    og = (ocat[...] * g_ref[...].astype(F32)).astype(BF16)
    y = jnp.dot(og, wo_ref[...], preferred_element_type=F32)
    o_ref[...] = h_ref[...] + mod_ref[2:3, :] * y


def _attn_core(hs, mods, layer, q, k, v, g, sink, w_out):
    bsz, t, d = hs.shape
    kvw = ATTN_KV_HEADS * LANES
    nb = ATTN_BLOCK
    ctx_blocks = CTX_LEN // nb
    n_blocks = t // nb
    n_lat = n_blocks - ctx_blocks
    blk_spec = lambda width, imap: pl.BlockSpec((None, nb, width), imap)
    own = lambda b, i: (b, i, 0)
    prev = lambda b, i: (b, jnp.maximum(i - 1, ctx_blocks), 0)
    nxt = lambda b, i: (b, jnp.minimum(jnp.maximum(i + 1, ctx_blocks), n_blocks - 1), 0)
    ctx_spec = pl.BlockSpec((None, CTX_LEN, kvw), lambda b, i: (b, 0, 0))
    kv_specs = [blk_spec(kvw, prev), blk_spec(kvw, own), blk_spec(kvw, nxt), ctx_spec]
    return pl.pallas_call(
        functools.partial(_attn_core_kernel, n_lat),
        grid=(bsz, n_blocks),
        in_specs=[
            pl.BlockSpec(memory_space=pltpu.SMEM),
            blk_spec(d, own),
            *kv_specs, *kv_specs,
            blk_spec(d, own),
            blk_spec(d, own),
            pl.BlockSpec((None, None, 3, d), lambda b, i: (layer, _mod_row(i < ctx_blocks, b), 0, 0)),
            pl.BlockSpec((d, d), lambda b, i: (0, 0)),
        ],
        out_specs=blk_spec(d, own),
        out_shape=jax.ShapeDtypeStruct((bsz, t, d), F32),
        scratch_shapes=[
            pltpu.VMEM((3 * nb + CTX_LEN, kvw), BF16),
            pltpu.VMEM((ATTN_KV_HEADS, 2 * (3 * nb + CTX_LEN), LANES), BF16),
            pltpu.VMEM((nb, d), F32),
        ],
        compiler_params=pltpu.CompilerParams(
            dimension_semantics=("parallel", "parallel"), vmem_limit_bytes=VMEM_LIMIT),
        name="attn_core",
    )(sink, q, k, k, k, k, v, v, v, v, g, hs, mods, w_out)


def _ssd_in_kernel(h_ref, hp_ref, hn_ref, mod_ref, wz_ref, wx_ref, wdt_ref, cw_ref, cb_ref, dtb_ref,
                   z_ref, xs_ref, bm_ref, cm_ref, dt_ref, rot_ref):
    blk = pl.program_id(1)
    n_tiles = pl.num_programs(1)
    ctx_tiles = CTX_LEN // ROW_TILE
    halo = SUBLANES
    tm = ROW_TILE
    width = xs_ref.shape[-1]
    gs_width = bm_ref.shape[-1]

    prev_ok = (blk > ctx_tiles) | ((blk > 0) & (blk < ctx_tiles))
    next_ok = ((blk >= ctx_tiles) & (blk < n_tiles - 1)) | (blk < ctx_tiles - 1)
    u_main = _norm_modulate(h_ref[...], mod_ref)
    u_all = jnp.concatenate(
        [_norm_modulate(hp_ref[...], mod_ref) * prev_ok.astype(F32), u_main,
         _norm_modulate(hn_ref[...], mod_ref) * next_ok.astype(F32)], axis=0).astype(BF16)
    ub = u_main.astype(BF16)

    z_ref[...] = jnp.dot(ub, wz_ref[...], preferred_element_type=F32).astype(BF16)
    dt_raw = jnp.dot(ub, wdt_ref[...], preferred_element_type=F32) + dtb_ref[...]
    dt_ref[...] = jnp.maximum(dt_raw, 0.0) + jnp.log1p(jnp.exp(-jnp.abs(dt_raw)))

    pad = SSD_CONV // 2
    n_rows = tm + 2 * halo
    sub = lax.broadcasted_iota(jnp.int32, (1, SUBLANES, CONV_COL_TILE), 1)
    for c in range(wx_ref.shape[-1] // CONV_COL_TILE):
        cs = slice(c * CONV_COL_TILE, (c + 1) * CONV_COL_TILE)
        pre = jnp.dot(u_all, wx_ref[:, cs], preferred_element_type=F32)
        tiles = pre.reshape(n_rows // SUBLANES, SUBLANES, CONV_COL_TILE)
        n_out = tm // SUBLANES
        acc = cb_ref[:, cs] + cw_ref[pad:pad + 1, cs] * tiles[1:1 + n_out]
        for kk in range(SSD_CONV):
            off = kk - pad
            if off == 0:
                continue
            rot_ref[kk] = pltpu.roll(tiles, (-off) % SUBLANES, 1)
            same = (sub + off < SUBLANES) if off > 0 else (sub + off >= 0)
            nbr = 1 + (1 if off > 0 else -1)
            acc = acc + cw_ref[kk:kk + 1, cs] * jnp.where(
                same, rot_ref[kk, 1:1 + n_out], rot_ref[kk, nbr:nbr + n_out])
        out = _silu(acc).reshape(tm, CONV_COL_TILE).astype(BF16)
        lo = c * CONV_COL_TILE
        if lo < width:
            xs_ref[:, lo:lo + CONV_COL_TILE] = out
        elif lo < width + gs_width:
            bm_ref[:, lo - width:lo - width + CONV_COL_TILE] = out
        else:
            cm_ref[:, lo - width - gs_width:lo - width - gs_width + CONV_COL_TILE] = out


def _ssd_in(hs, mods, layer, w_z, w_xbc, w_dt, conv_w, conv_b, dt_bias):
    bsz, t, d = hs.shape
    width = w_z.shape[1]
    gs_width = SSD_GROUPS * SSD_STATE
    n_tiles = t // ROW_TILE
    ctx_tiles = CTX_LEN // ROW_TILE
    per_tile = ROW_TILE // SUBLANES
    n_halo_blocks = t // SUBLANES
    row_spec = lambda w_: pl.BlockSpec((None, ROW_TILE, w_), lambda b, i: (b, i, 0))
    full = lambda a: pl.BlockSpec(a.shape, lambda b, i: (0,) * a.ndim)
    return pl.pallas_call(
        _ssd_in_kernel,
        grid=(bsz, n_tiles),
        in_specs=[
            row_spec(d),
            pl.BlockSpec((None, SUBLANES, d), lambda b, i: (b, jnp.maximum(i * per_tile - 1, 0), 0)),
            pl.BlockSpec((None, SUBLANES, d),
                         lambda b, i: (b, jnp.minimum((i + 1) * per_tile, n_halo_blocks - 1), 0)),
            pl.BlockSpec((None, None, 3, d), lambda b, i: (layer, _mod_row(i < ctx_tiles, b), 0, 0)),
            full(w_z), full(w_xbc), full(w_dt), full(conv_w), full(conv_b), full(dt_bias),
        ],
        out_specs=[row_spec(width), row_spec(width), row_spec(gs_width), row_spec(gs_width),
                   row_spec(LANES)],
        out_shape=[
            jax.ShapeDtypeStruct((bsz, t, width), BF16),
            jax.ShapeDtypeStruct((bsz, t, width), BF16),
            jax.ShapeDtypeStruct((bsz, t, gs_width), BF16),
            jax.ShapeDtypeStruct((bsz, t, gs_width), BF16),
            jax.ShapeDtypeStruct((bsz, t, LANES), F32),
        ],
        scratch_shapes=[
            pltpu.VMEM((SSD_CONV, ROW_TILE // SUBLANES + 2, SUBLANES, CONV_COL_TILE), F32)],
        compiler_params=pltpu.CompilerParams(
            dimension_semantics=("parallel", "parallel"), vmem_limit_bytes=VMEM_LIMIT),
        name="ssd_in",
    )(hs, hs, hs, mods, w_z, w_xbc, w_dt, conv_w, conv_b, dt_bias)


def _ssd_chunk(direction, xs_ref, bm_ref, cm_ref, dt_ref, alog_ref, st_ref, emit, skip_ref=None):
    q = SSD_CHUNK
    log2e = math.log2(math.e)
    dt = dt_ref[...]
    da = dt * (-jnp.exp(alog_ref[...]))
    row = lax.broadcasted_iota(jnp.int32, (q, q), 0)
    col = lax.broadcasted_iota(jnp.int32, (q, q), 1)
    seen = (row >= col) if direction == 0 else (col >= row)
    a = jnp.dot(seen.astype(F32), da, preferred_element_type=F32,
                precision=lax.Precision.HIGHEST)
    last = q - 1 if direction == 0 else 0
    a_tot = a[last:last + 1, :]
    w_end = jnp.exp(a_tot - a) * dt
    dec = jnp.exp(a_tot)
    a2 = a * log2e
    src2_t = ((a - jnp.log(dt)) * log2e).T
    left = col < HEAD_DIM

    for g in range(SSD_GROUPS):
        gsl = slice(g * SSD_STATE, (g + 1) * SSD_STATE)
        cmg = cm_ref[:, gsl]
        bmg = bm_ref[:, gsl]
        cb = lax.dot_general(cmg, bmg, (((1,), (1,)), ((), ())), preferred_element_type=F32)
        h_in = st_ref[g]
        y_off = jnp.dot(cmg, h_in.astype(BF16), preferred_element_type=F32)
        ys, xws, decs = [], [], []
        for pr in range(SSD_HEADS_PER_GROUP // 2):
            pair = g * (SSD_HEADS_PER_GROUP // 2) + pr
            c0 = direction * SSD_HEADS + 2 * pair
            psl = slice(pair * LANES, (pair + 1) * LANES)
            xp = xs_ref[:, psl]
            ws, eas = [], []
            for c in (c0, c0 + 1):
                a_col = jnp.broadcast_to(a2[:, c:c + 1], (q, q))
                lmat = jnp.exp2(jnp.where(seen, a_col - src2_t[c:c + 1, :], -jnp.inf))
                ws.append((cb * lmat).astype(BF16))
                eas.append(jnp.exp2(a_col))
            zero = jnp.zeros_like(xp)
            x_diag = jnp.concatenate([jnp.where(left, xp, zero), jnp.where(left, zero, xp)], axis=0)
            y_diag = jnp.dot(jnp.concatenate(ws, axis=1), x_diag, preferred_element_type=F32)
            y = y_diag + y_off[:, pr * LANES:(pr + 1) * LANES] * jnp.where(left, eas[0], eas[1])
            xf = xp.astype(F32)
            if skip_ref is not None:
                y = y + skip_ref[:, psl] * xf
            ys.append(y)
            idx = jnp.where(left, c0, c0 + 1)
            xws.append((xf * jnp.take_along_axis(w_end, idx, axis=1)).astype(BF16))
            decs.append(jnp.where(left[0:1, :], jnp.broadcast_to(dec[:, c0:c0 + 1], (1, LANES)),
                                  jnp.broadcast_to(dec[:, c0 + 1:c0 + 2], (1, LANES))))
        emit(g, jnp.concatenate(ys, axis=1))
        new = lax.dot_general(bmg, jnp.concatenate(xws, axis=1), (((0,), (0,)), ((), ())),
                              preferred_element_type=F32)
        st_ref[g] = h_in * jnp.concatenate(decs, axis=1) + new


def _ssd_fwd_kernel(xs_ref, bm_ref, cm_ref, dt_ref, alog_ref, dsk_ref, y_ref, st_ref):
    @pl.when(pl.program_id(1) == 0)
    def _():
        st_ref[...] = jnp.zeros_like(st_ref)

    def emit(g, y):
        y_ref[:, g * SSD_GROUP_WIDTH:(g + 1) * SSD_GROUP_WIDTH] = y

    _ssd_chunk(0, xs_ref, bm_ref, cm_ref, dt_ref, alog_ref, st_ref, emit, skip_ref=dsk_ref)


def _ssd_bwd_kernel(xs_ref, bm_ref, cm_ref, dt_ref, alog_ref, yf_ref, z_ref, h_ref, mod_ref, nw_ref,
                    wo_ref, o_ref, st_ref, u_ref):
    @pl.when(pl.program_id(1) == 0)
    def _():
        st_ref[...] = jnp.zeros_like(st_ref)

    def emit(g, y):
        sl = slice(g * SSD_GROUP_WIDTH, (g + 1) * SSD_GROUP_WIDTH)
        u = (y + yf_ref[:, sl]) * _silu(z_ref[:, sl].astype(F32))
        ms = jnp.mean(u * u, axis=-1, keepdims=True)
        u_ref[:, sl] = (u * lax.rsqrt(ms + EPS) * nw_ref[:, sl]).astype(BF16)

    _ssd_chunk(1, xs_ref, bm_ref, cm_ref, dt_ref, alog_ref, st_ref, emit)
    y = jnp.dot(u_ref[...], wo_ref[...], preferred_element_type=F32)
    o_ref[...] = h_ref[...] + mod_ref[2:3, :] * y


def _ssd_scans(hs, mods, layer, z, xs, bm, cm, dt, a_log, d_skip, norm_w, w_out):
    bsz, t, d = hs.shape
    width = xs.shape[-1]
    gs_width = bm.shape[-1]
    q = SSD_CHUNK
    n_chunks = t // q
    ctx_chunks = CTX_LEN // q
    fwd = lambda b, i: (b, i, 0)
    bwd_chunk = lambda i: jnp.where(i < ctx_chunks, ctx_chunks - 1 - i, n_chunks - 1 + ctx_chunks - i)
    bwd = lambda b, i: (b, bwd_chunk(i), 0)
    const = lambda a: pl.BlockSpec(a.shape, lambda b, i: (0,) * a.ndim)
    state = pltpu.VMEM((SSD_GROUPS, SSD_STATE, SSD_GROUP_WIDTH), F32)
    chunk_specs = lambda imap: [
        pl.BlockSpec((None, q, width), imap), pl.BlockSpec((None, q, gs_width), imap),
        pl.BlockSpec((None, q, gs_width), imap), pl.BlockSpec((None, q, LANES), imap)]

    y_f = pl.pallas_call(
        _ssd_fwd_kernel,
        grid=(bsz, n_chunks),
        in_specs=[*chunk_specs(fwd), const(a_log), const(d_skip)],
        out_specs=pl.BlockSpec((None, q, width), fwd),
        out_shape=jax.ShapeDtypeStruct((bsz, t, width), F32),
        scratch_shapes=[state],
        compiler_params=pltpu.CompilerParams(
            dimension_semantics=("parallel", "arbitrary"), vmem_limit_bytes=VMEM_LIMIT),
        name="ssd_fwd",
    )(xs, bm, cm, dt, a_log, d_skip)

    return pl.pallas_call(
        _ssd_bwd_kernel,
        grid=(bsz, n_chunks),
        in_specs=[
            *chunk_specs(bwd), const(a_log),
            pl.BlockSpec((None, q, width), bwd),
            pl.BlockSpec((None, q, width), bwd),
            pl.BlockSpec((None, q, d), bwd),
            pl.BlockSpec((None, None, 3, d),
                         lambda b, i: (layer, _mod_row(bwd_chunk(i) < ctx_chunks, b), 0, 0)),
            const(norm_w), const(w_out),
        ],
        out_specs=pl.BlockSpec((None, q, d), bwd),
        out_shape=jax.ShapeDtypeStruct((bsz, t, d), F32),
        scratch_shapes=[state, pltpu.VMEM((q, width), BF16)],
        compiler_params=pltpu.CompilerParams(
            dimension_semantics=("parallel", "arbitrary"), vmem_limit_bytes=VMEM_LIMIT),
        name="ssd_bwd",
    )(xs, bm, cm, dt, a_log, y_f, z, hs, mods, norm_w, w_out)


def _final_norm_kernel(h_ref, w_ref, o_ref):
    x = h_ref[...]
    ms = jnp.mean(x * x, axis=-1, keepdims=True)
    o_ref[...] = x * lax.rsqrt(ms + EPS) * w_ref[...]


def _final_norm(hs, w):
    bsz, t, d = hs.shape
    ctx_tiles = CTX_LEN // ROW_TILE
    n_lat = t - CTX_LEN
    return pl.pallas_call(
        _final_norm_kernel,
        grid=(bsz, n_lat // ROW_TILE),
        in_specs=[pl.BlockSpec((None, ROW_TILE, d), lambda b, i: (b, i + ctx_tiles, 0)),
                  pl.BlockSpec((1, d), lambda b, i: (0, 0))],
        out_specs=pl.BlockSpec((None, ROW_TILE, d), lambda b, i: (b, i, 0)),
        out_shape=jax.ShapeDtypeStruct((bsz, n_lat, d), F32),
        compiler_params=pltpu.CompilerParams(
            dimension_semantics=("parallel", "parallel"), vmem_limit_bytes=VMEM_LIMIT),
        name="final_norm",
    )(hs, w)


def _rope_tables(seq_len):
    quarter = HEAD_DIM // 4
    pos = jnp.arange(seq_len, dtype=jnp.int32)
    row = (pos // GRID_W).astype(F32)
    col = (pos % GRID_W).astype(F32)
    inv_freq = ROPE_BASE ** (-jnp.arange(0, 2 * quarter, 2, dtype=F32) / (2 * quarter))
    lane = jnp.arange(LANES)
    in_head = lane % HEAD_DIM
    p = jnp.where((in_head < HEAD_DIM // 2)[None, :], row[:, None], col[:, None])
    ang = p * inv_freq[lane % quarter][None, :]
    first = ((lane % (2 * quarter)) < quarter)[None, :]
    cos = jnp.cos(ang)
    sin = jnp.sin(ang)
    sa = jnp.where(first, -sin, 0.0)
    sb = jnp.where(first, 0.0, sin)
    ident = jnp.ones((CTX_LEN, LANES), F32)
    zeros = jnp.zeros((CTX_LEN, LANES), F32)
    return (jnp.concatenate([ident, cos], 0), jnp.concatenate([zeros, sa], 0),
            jnp.concatenate([zeros, sb], 0))


def kernel(x, c, ctx, c_ctx, w_ada, b_ada, attn_w_in, attn_sink, attn_w_out, ssd_w_in, ssd_conv_w,
           ssd_conv_b, ssd_dt_bias, ssd_a_log, ssd_d, ssd_norm_w, ssd_w_out, final_norm_w):
    bsz, seq_len, d = x.shape
    depth = w_ada.shape[0]
    assert ctx.shape[1] == CTX_LEN and CTX_LEN % ROW_TILE == 0 and seq_len % ROW_TILE == 0
    assert bsz <= 8

    cond = jnp.zeros((16, d), F32).at[:bsz].set(c).at[8].set(c_ctx)
    mods = _modulation_tables(cond, w_ada, b_ada).reshape(depth, 16, 3, d)
    cos, sa, sb = _rope_tables(seq_len)
    hs = jnp.concatenate([ctx, x], axis=1)

    ssd_width = ssd_w_out.shape[1]
    conv_dim = ssd_conv_w.shape[-1]
    pad_lanes = lambda a: jnp.pad(a, ((0, 0), (0, LANES - a.shape[-1])))

    for i in range(depth):
        j = i // 2
        if i % 2 == 0:
            q, k, v, g = _attn_in(hs, mods, i, attn_w_in[j].astype(BF16), cos, sa, sb)
            hs = _attn_core(hs, mods, i, q, k, v, g, attn_sink[j], attn_w_out[j].astype(BF16))
        else:
            w_in = ssd_w_in[j]
            w_z = w_in[:, :ssd_width].astype(BF16)
            w_xbc = w_in[:, ssd_width:ssd_width + conv_dim].astype(BF16)
            w_dt = pad_lanes(w_in[:, ssd_width + conv_dim:]).astype(BF16)
            z, xs, bm, cm, dt = _ssd_in(
                hs, mods, i, w_z, w_xbc, w_dt, ssd_conv_w[j], ssd_conv_b[j][None, :],
                pad_lanes(ssd_dt_bias[j].reshape(1, -1)))
            hs = _ssd_scans(
                hs, mods, i, z, xs, bm, cm, dt, pad_lanes(ssd_a_log[j].reshape(1, -1)),
                jnp.repeat(ssd_d[j], HEAD_DIM)[None, :], ssd_norm_w[j][None, :],
                ssd_w_out[j].astype(BF16))
    return _final_norm(hs, final_norm_w[None, :])
```

```python
import functools
import math

import jax
import jax.numpy as jnp
from jax import lax
from jax.experimental import pallas as pl
from jax.experimental.pallas import tpu as pltpu

F32 = jnp.float32
BF16 = jnp.bfloat16

EPS = 1e-6
CTX_LEN = 256
GRID_W = 64
ROPE_BASE = 10000.0

HEAD_DIM = 64
ATTN_HEADS = 16
ATTN_KV_HEADS = 4
ATTN_GROUP = ATTN_HEADS // ATTN_KV_HEADS
ATTN_BLOCK = 128
LANES = 128
SUBLANES = 8

SSD_HEADS = 32
SSD_GROUPS = 8
SSD_HEADS_PER_GROUP = SSD_HEADS // SSD_GROUPS
SSD_STATE = 128
SSD_CONV = 5
SSD_CHUNK = 128
SSD_GROUP_WIDTH = SSD_HEADS_PER_GROUP * HEAD_DIM

ROW_TILE = 256
CONV_COL_TILE = 512
HALO = 16
MASKED = -1e30
VMEM_LIMIT = 56 * 1024 * 1024


def _silu(x):
    return x / (1.0 + jnp.exp2(x * -math.log2(math.e)))


def _norm_modulate(x, mod_ref):
    ms = jnp.mean(x * x, axis=-1, keepdims=True)
    return x * lax.rsqrt(ms + EPS) * (1.0 + mod_ref[1:2, :]) + mod_ref[0:1, :]


def _mod_row(is_ctx, b):
    return jnp.where(is_ctx, 8, b)


def _mod_kernel(cond_ref, w_ref, b_ref, o_ref):
    cnd = cond_ref[...]
    o_ref[...] = jnp.dot(_silu(cnd), w_ref[...], preferred_element_type=F32,
                         precision=lax.Precision.HIGHEST) + b_ref[...]


def _modulation_tables(cond, w_ada, b_ada):
    depth, d, d3 = w_ada.shape
    return pl.pallas_call(
        _mod_kernel,
        grid=(depth, d3 // d),
        in_specs=[
            pl.BlockSpec((16, d), lambda l, j: (0, 0)),
            pl.BlockSpec((None, d, d), lambda l, j: (l, 0, j)),
            pl.BlockSpec((None, 1, d), lambda l, j: (l, 0, j)),
        ],
        out_specs=pl.BlockSpec((None, 16, d), lambda l, j: (l, 0, j)),
        out_shape=jax.ShapeDtypeStruct((depth, 16, d3), F32),
        compiler_params=pltpu.CompilerParams(vmem_limit_bytes=VMEM_LIMIT),
        name="adaln_tables",
    )(cond, w_ada, b_ada.reshape(depth, 1, d3))


def _attn_in_kernel(pending, *refs):
    if pending:
        h_ref, delta_ref, *refs, hnew_ref = refs
        h = h_ref[...] + delta_ref[...].astype(F32)
        hnew_ref[...] = h
    else:
        h_ref, *refs = refs
        h = h_ref[...]
    mod_ref, w_ref, cos_ref, sa_ref, sb_ref, q_ref, k_ref, v_ref, g_ref = refs
    d = h_ref.shape[-1]
    kvw = ATTN_KV_HEADS * HEAD_DIM
    ub = _norm_modulate(h, mod_ref).astype(BF16)
    cos = cos_ref[...]
    sa = sa_ref[...]
    sb = sb_ref[...]

    def rope(t):
        return t * cos + pltpu.roll(t, LANES - 16, 1) * sa + pltpu.roll(t, 16, 1) * sb

    left = lax.broadcasted_iota(jnp.int32, (h_ref.shape[0], LANES), 1) < HEAD_DIM

    def store_doubled(ref, j, t):
        sw = pltpu.roll(t, HEAD_DIM, 1)
        ref[:, (2 * j) * LANES:(2 * j + 1) * LANES] = jnp.where(left, t, sw).astype(BF16)
        ref[:, (2 * j + 1) * LANES:(2 * j + 2) * LANES] = jnp.where(left, sw, t).astype(BF16)

    q = jnp.dot(ub, w_ref[:, 0:d], preferred_element_type=F32)
    scale = HEAD_DIM ** -0.5 * math.log2(math.e)
    for j in range(d // LANES):
        sl = slice(j * LANES, (j + 1) * LANES)
        q_ref[:, sl] = (rope(q[:, sl]) * scale).astype(BF16)
    k = jnp.dot(ub, w_ref[:, d:d + kvw], preferred_element_type=F32)
    v = jnp.dot(ub, w_ref[:, d + kvw:d + 2 * kvw], preferred_element_type=F32)
    for j in range(kvw // LANES):
        sl = slice(j * LANES, (j + 1) * LANES)
        store_doubled(k_ref, j, rope(k[:, sl]))
        store_doubled(v_ref, j, v[:, sl])
    g = jnp.dot(ub, w_ref[:, d + 2 * kvw:], preferred_element_type=F32)
    g_ref[...] = _silu(g).astype(BF16)


def _attn_in(hs, pending, mods, layer, w_in, cos, sa, sb):
    bsz, t, d = hs.shape
    kvw = ATTN_KV_HEADS * HEAD_DIM
    n_in = w_in.shape[1]
    ctx_tiles = CTX_LEN // ROW_TILE
    row_spec = lambda width: pl.BlockSpec((None, ROW_TILE, width), lambda b, i: (b, i, 0))
    tab_spec = pl.BlockSpec((ROW_TILE, LANES), lambda b, i: (i, 0))
    mod_spec = lambda l: pl.BlockSpec(
        (None, None, 3, d), lambda b, i: (l, _mod_row(i < ctx_tiles, b), 0, 0))
    const = lambda a: pl.BlockSpec(a.shape, lambda b, i: (0,) * a.ndim)
    in_specs = [row_spec(d)]
    args = [hs]
    out_specs = [row_spec(d), row_spec(2 * kvw), row_spec(2 * kvw), row_spec(d)]
    out_shape = [
        jax.ShapeDtypeStruct((bsz, t, d), BF16),
        jax.ShapeDtypeStruct((bsz, t, 2 * kvw), BF16),
        jax.ShapeDtypeStruct((bsz, t, 2 * kvw), BF16),
        jax.ShapeDtypeStruct((bsz, t, d), BF16),
    ]
    if pending is not None:
        in_specs.append(row_spec(d))
        args.append(pending)
        out_specs.append(row_spec(d))
        out_shape.append(jax.ShapeDtypeStruct((bsz, t, d), F32))
    in_specs += [mod_spec(layer), const(w_in), tab_spec, tab_spec, tab_spec]
    args += [mods, w_in, cos, sa, sb]
    outs = pl.pallas_call(
        functools.partial(_attn_in_kernel, pending is not None),
        grid=(bsz, t // ROW_TILE),
        in_specs=in_specs,
        out_specs=out_specs,
        out_shape=out_shape,
        compiler_params=pltpu.CompilerParams(
            dimension_semantics=("parallel", "parallel"), vmem_limit_bytes=VMEM_LIMIT),
        name="attn_in",
    )(*args)
    if pending is None:
        return (hs, *outs)
    return (outs[4], *outs[:4])


def _attn_core_kernel(n_lat_blocks, sink_ref, q_ref, kp_ref, kc_ref, kn_ref, kx_ref,
                      vp_ref, vc_ref, vn_ref, vx_ref, g_ref, mod_ref, wo_ref, o_ref,
                      kcat, vdiag):
    blk = pl.program_id(1)
    ctx_blocks = CTX_LEN // ATTN_BLOCK
    j = blk - ctx_blocks
    nb = ATTN_BLOCK
    n_keys = 3 * nb + CTX_LEN
    pairs = ATTN_GROUP // 2
    log2e = math.log2(math.e)
    nt_dims = (((1,), (1,)), ((), ()))

    def left(rows):
        return lax.broadcasted_iota(jnp.int32, (rows, LANES), 1) < HEAD_DIM

    def split_heads(t):
        zero = jnp.zeros_like(t)
        return jnp.where(left(t.shape[0]), t, zero), jnp.where(left(t.shape[0]), zero, t)

    def stacked_queries(kh):
        rows = []
        for pr in range(pairs):
            rows.extend(split_heads(q_ref[:, (kh * pairs + pr) * LANES:(kh * pairs + pr + 1) * LANES]))
        return jnp.concatenate(rows, axis=0)

    def softmax(kh, s):
        sink = jnp.concatenate(
            [jnp.full((nb, 1), sink_ref[kh * ATTN_GROUP + r] * log2e, F32) for r in range(ATTN_GROUP)],
            axis=0)
        m = jnp.maximum(jnp.max(s, axis=-1, keepdims=True), sink)
        p = jnp.exp2(s - m)
        den = jnp.sum(p, axis=-1, keepdims=True) + jnp.exp2(sink - m)
        return p.astype(BF16), 1.0 / den

    def weighted_values(kh, pb, r, n_keys):
        lhs = jnp.concatenate(
            [jnp.concatenate([pb[2 * pr * nb:(2 * pr + 1) * nb], pb[(2 * pr + 1) * nb:(2 * pr + 2) * nb]],
                             axis=1) for pr in range(pairs)], axis=0)
        o = jnp.dot(lhs, vdiag[kh, 0:2 * n_keys, :], preferred_element_type=F32)
        gated = []
        for pr in range(pairs):
            lo, mid, hi = 2 * pr * nb, (2 * pr + 1) * nb, (2 * pr + 2) * nb
            rr = jnp.where(left(nb), r[lo:mid], r[mid:hi])
            sl = slice((kh * pairs + pr) * LANES, (kh * pairs + pr + 1) * LANES)
            gated.append((o[pr * nb:(pr + 1) * nb] * rr * g_ref[:, sl].astype(F32)).astype(BF16))
        rows = slice(kh * pairs * LANES, (kh + 1) * pairs * LANES)
        return jnp.dot(jnp.concatenate(gated, axis=1), wo_ref[rows, :], preferred_element_type=F32)

    def attend(key_blocks, bias):
        n_keys = sum(k_ref.shape[0] for k_ref, _ in key_blocks)
        r0 = 0
        for k_ref, v_ref in key_blocks:
            r1 = r0 + k_ref.shape[0]
            kcat[r0:r1, :] = k_ref[...]
            for kh in range(ATTN_KV_HEADS):
                top, bot = split_heads(v_ref[:, kh * LANES:(kh + 1) * LANES])
                vdiag[kh, r0:r1, :] = top
                vdiag[kh, n_keys + r0:n_keys + r1, :] = bot
            r0 = r1

        def logits(kh):
            s = lax.dot_general(stacked_queries(kh), kcat[0:n_keys, kh * LANES:(kh + 1) * LANES],
                                nt_dims, preferred_element_type=F32)
            if not bias:
                return s
            return jnp.concatenate(
                [s[:, i * nb:(i + 1) * nb] + bias[i] if i in bias else s[:, i * nb:(i + 1) * nb]
                 for i in range(n_keys // nb)], axis=1)

        s_next = logits(0)
        pending = None
        y = None
        for kh in range(ATTN_KV_HEADS):
            s = s_next
            if kh + 1 < ATTN_KV_HEADS:
                s_next = logits(kh + 1)
            pb, r = softmax(kh, s)
            if pending is not None:
                part = weighted_values(*pending, n_keys)
                y = part if y is None else y + part
            pending = (kh, pb, r)
        y = y + weighted_values(*pending, n_keys)
        o_ref[...] = (mod_ref[2:3, :] * y).astype(BF16)

    @pl.when(blk >= ctx_blocks)
    def _():
        row = lax.broadcasted_iota(jnp.int32, (nb, nb), 0)
        col = lax.broadcasted_iota(jnp.int32, (nb, nb), 1)
        bias_prev = jnp.where((col >= row) & (j > 0), 0.0, MASKED).astype(F32)
        bias_next = jnp.where((col <= row) & (j < n_lat_blocks - 1), 0.0, MASKED).astype(F32)
        attend(((kp_ref, vp_ref), (kc_ref, vc_ref), (kn_ref, vn_ref), (kx_ref, vx_ref)),
               {0: jnp.concatenate([bias_prev] * ATTN_GROUP, axis=0),
                2: jnp.concatenate([bias_next] * ATTN_GROUP, axis=0)})

    @pl.when(blk < ctx_blocks)
    def _():
        attend(((kx_ref, vx_ref),), {})


def _attn_core(mods, layer, q, k, v, g, sink, w_out):
    bsz, t, d = q.shape
    kvw = ATTN_KV_HEADS * LANES
    nb = ATTN_BLOCK
    ctx_blocks = CTX_LEN // nb
    n_blocks = t // nb
    n_lat = n_blocks - ctx_blocks
    blk_spec = lambda width, imap: pl.BlockSpec((None, nb, width), imap)
    own = lambda b, i: (b, i, 0)
    prev = lambda b, i: (b, jnp.maximum(i - 1, ctx_blocks), 0)
    nxt = lambda b, i: (b, jnp.minimum(jnp.maximum(i + 1, ctx_blocks), n_blocks - 1), 0)
    ctx_spec = pl.BlockSpec((None, CTX_LEN, kvw), lambda b, i: (b, 0, 0))
    kv_specs = [blk_spec(kvw, prev), blk_spec(kvw, own), blk_spec(kvw, nxt), ctx_spec]
    return pl.pallas_call(
        functools.partial(_attn_core_kernel, n_lat),
        grid=(bsz, n_blocks),
        in_specs=[
            pl.BlockSpec(memory_space=pltpu.SMEM),
            blk_spec(d, own),
            *kv_specs, *kv_specs,
            blk_spec(d, own),
            pl.BlockSpec((None, None, 3, d), lambda b, i: (layer, _mod_row(i < ctx_blocks, b), 0, 0)),
            pl.BlockSpec((d, d), lambda b, i: (0, 0)),
        ],
        out_specs=blk_spec(d, own),
        out_shape=jax.ShapeDtypeStruct((bsz, t, d), BF16),
        scratch_shapes=[
            pltpu.VMEM((3 * nb + CTX_LEN, kvw), BF16),
            pltpu.VMEM((ATTN_KV_HEADS, 2 * (3 * nb + CTX_LEN), LANES), BF16),
        ],
        compiler_params=pltpu.CompilerParams(
            dimension_semantics=("parallel", "parallel"), vmem_limit_bytes=VMEM_LIMIT),
        name="attn_core",
    )(sink, q, k, k, k, k, v, v, v, v, g, mods, w_out)


def _ssd_in_kernel(h_ref, hp_ref, hn_ref, d_ref, dp_ref, dn_ref, mod_ref,
                   wz_ref, wx_ref, wdt_ref, cw_ref, cb_ref, dtb_ref,
                   hnew_ref, z_ref, xs_ref, bm_ref, cm_ref, dt_ref):
    blk = pl.program_id(1)
    n_tiles = pl.num_programs(1)
    ctx_tiles = CTX_LEN // ROW_TILE
    halo = HALO
    tm = ROW_TILE
    width = xs_ref.shape[-1]
    gs_width = bm_ref.shape[-1]

    h_all = jnp.concatenate(
        [hp_ref[...] + dp_ref[...].astype(F32), h_ref[...] + d_ref[...].astype(F32),
         hn_ref[...] + dn_ref[...].astype(F32)], axis=0)
    hnew_ref[...] = h_all[halo:halo + tm]

    prev_ok = (blk > ctx_tiles) | ((blk > 0) & (blk < ctx_tiles))
    next_ok = ((blk >= ctx_tiles) & (blk < n_tiles - 1)) | (blk < ctx_tiles - 1)
    u = _norm_modulate(h_all, mod_ref)
    u_main = u[halo:halo + tm]
    u_all = jnp.concatenate(
        [u[0:halo] * prev_ok.astype(F32), u_main, u[halo + tm:] * next_ok.astype(F32)],
        axis=0).astype(BF16)
    ub = u_main.astype(BF16)

    z_ref[...] = jnp.dot(ub, wz_ref[...], preferred_element_type=F32).astype(BF16)
    dt_raw = jnp.dot(ub, wdt_ref[...], preferred_element_type=F32) + dtb_ref[...]
    dt_ref[...] = jnp.maximum(dt_raw, 0.0) + jnp.log1p(jnp.exp(-jnp.abs(dt_raw)))

    pad = SSD_CONV // 2
    n_rows = tm + 2 * halo
    sub = lax.broadcasted_iota(jnp.int32, (1, SUBLANES, CONV_COL_TILE), 1)
    for c in range(wx_ref.shape[-1] // CONV_COL_TILE):
        cs = slice(c * CONV_COL_TILE, (c + 1) * CONV_COL_TILE)
        pre = jnp.dot(u_all, wx_ref[:, cs], preferred_element_type=F32)
        tiles = pre.reshape(n_rows // SUBLANES, SUBLANES, CONV_COL_TILE)
        n_out = tm // SUBLANES
        t0 = halo // SUBLANES
        acc = cb_ref[:, cs] + cw_ref[pad:pad + 1, cs] * tiles[t0:t0 + n_out]
        for kk in range(SSD_CONV):
            off = kk - pad
            if off == 0:
                continue
            rot = pltpu.roll(tiles, (-off) % SUBLANES, 1)
            same = (sub + off < SUBLANES) if off > 0 else (sub + off >= 0)
            nbr = t0 + (1 if off > 0 else -1)
            acc = acc + cw_ref[kk:kk + 1, cs] * jnp.where(
                same, rot[t0:t0 + n_out], rot[nbr:nbr + n_out])
        out = _silu(acc).reshape(tm, CONV_COL_TILE).astype(BF16)
        lo = c * CONV_COL_TILE
        if lo < width:
            xs_ref[:, lo:lo + CONV_COL_TILE] = out
        elif lo < width + gs_width:
            bm_ref[:, lo - width:lo - width + CONV_COL_TILE] = out
        else:
            cm_ref[:, lo - width - gs_width:lo - width - gs_width + CONV_COL_TILE] = out


def _ssd_in(hs, pending, mods, layer, w_z, w_xbc, w_dt, conv_w, conv_b, dt_bias):
    bsz, t, d = hs.shape
    width = w_z.shape[1]
    gs_width = SSD_GROUPS * SSD_STATE
    n_tiles = t // ROW_TILE
    ctx_tiles = CTX_LEN // ROW_TILE
    per_tile = ROW_TILE // HALO
    n_halo_blocks = t // HALO
    row_spec = lambda w_: pl.BlockSpec((None, ROW_TILE, w_), lambda b, i: (b, i, 0))
    halo_before = lambda w_: pl.BlockSpec(
        (None, HALO, w_), lambda b, i: (b, jnp.maximum(i * per_tile - 1, 0), 0))
    halo_after = lambda w_: pl.BlockSpec(
        (None, HALO, w_), lambda b, i: (b, jnp.minimum((i + 1) * per_tile, n_halo_blocks - 1), 0))
    mod_spec = lambda l: pl.BlockSpec(
        (None, None, 3, d), lambda b, i: (l, _mod_row(i < ctx_tiles, b), 0, 0))
    full = lambda a: pl.BlockSpec(a.shape, lambda b, i: (0,) * a.ndim)
    return pl.pallas_call(
        _ssd_in_kernel,
        grid=(bsz, n_tiles),
        in_specs=[
            row_spec(d), halo_before(d), halo_after(d),
            row_spec(d), halo_before(d), halo_after(d),
            mod_spec(layer),
            full(w_z), full(w_xbc), full(w_dt), full(conv_w), full(conv_b), full(dt_bias),
        ],
        out_specs=[row_spec(d), row_spec(width), row_spec(width), row_spec(gs_width),
                   row_spec(gs_width), row_spec(LANES)],
        out_shape=[
            jax.ShapeDtypeStruct((bsz, t, d), F32),
            jax.ShapeDtypeStruct((bsz, t, width), BF16),
            jax.ShapeDtypeStruct((bsz, t, width), BF16),
            jax.ShapeDtypeStruct((bsz, t, gs_width), BF16),
            jax.ShapeDtypeStruct((bsz, t, gs_width), BF16),
            jax.ShapeDtypeStruct((bsz, t, LANES), F32),
        ],
        compiler_params=pltpu.CompilerParams(
            dimension_semantics=("parallel", "parallel"), vmem_limit_bytes=VMEM_LIMIT),
        name="ssd_in",
    )(hs, hs, hs, pending, pending, pending, mods, w_z, w_xbc, w_dt, conv_w, conv_b, dt_bias)


def _ssd_chunk(direction, xs_ref, bm_ref, cm_ref, dt_ref, alog_ref, st_ref, emit, skip_ref=None):
    q = SSD_CHUNK
    log2e = math.log2(math.e)
    dt = dt_ref[...]
    da = dt * (-jnp.exp(alog_ref[...]))
    row = lax.broadcasted_iota(jnp.int32, (q, q), 0)
    col = lax.broadcasted_iota(jnp.int32, (q, q), 1)
    seen = (row >= col) if direction == 0 else (col >= row)
    a = jnp.dot(seen.astype(F32), da, preferred_element_type=F32,
                precision=lax.Precision.HIGHEST)
    last = q - 1 if direction == 0 else 0
    a_tot = a[last:last + 1, :]
    w_end = jnp.exp(a_tot - a) * dt
    dec = jnp.exp(a_tot)
    a2 = a * log2e
    src2_t = ((a - jnp.log(dt)) * log2e).T
    left = col < HEAD_DIM

    for g in range(SSD_GROUPS):
        gsl = slice(g * SSD_STATE, (g + 1) * SSD_STATE)
        cmg = cm_ref[:, gsl]
        bmg = bm_ref[:, gsl]
        cb = lax.dot_general(cmg, bmg, (((1,), (1,)), ((), ())), preferred_element_type=F32)
        h_in = st_ref[g]
        y_off = jnp.dot(cmg, h_in.astype(BF16), preferred_element_type=F32)
        ys, xws, decs = [], [], []
        for pr in range(SSD_HEADS_PER_GROUP // 2):
            pair = g * (SSD_HEADS_PER_GROUP // 2) + pr
            c0 = direction * SSD_HEADS + 2 * pair
            psl = slice(pair * LANES, (pair + 1) * LANES)
            xp = xs_ref[:, psl]
            ws, eas = [], []
            for c in (c0, c0 + 1):
                a_col = jnp.broadcast_to(a2[:, c:c + 1], (q, q))
                lmat = jnp.exp2(jnp.where(seen, a_col - src2_t[c:c + 1, :], -jnp.inf))
                ws.append((cb * lmat).astype(BF16))
                eas.append(jnp.exp2(a_col))
            zero = jnp.zeros_like(xp)
            x_diag = jnp.concatenate([jnp.where(left, xp, zero), jnp.where(left, zero, xp)], axis=0)
            y_diag = jnp.dot(jnp.concatenate(ws, axis=1), x_diag, preferred_element_type=F32)
            y = y_diag + y_off[:, pr * LANES:(pr + 1) * LANES] * jnp.where(left, eas[0], eas[1])
            xf = xp.astype(F32)
            if skip_ref is not None:
                y = y + skip_ref[:, psl] * xf
            ys.append(y)
            idx = jnp.where(left, c0, c0 + 1)
            xws.append((xf * jnp.take_along_axis(w_end, idx, axis=1)).astype(BF16))
            decs.append(jnp.where(left[0:1, :], jnp.broadcast_to(dec[:, c0:c0 + 1], (1, LANES)),
                                  jnp.broadcast_to(dec[:, c0 + 1:c0 + 2], (1, LANES))))
        emit(g, jnp.concatenate(ys, axis=1))
        new = lax.dot_general(bmg, jnp.concatenate(xws, axis=1), (((0,), (0,)), ((), ())),
                              preferred_element_type=F32)
        st_ref[g] = h_in * jnp.concatenate(decs, axis=1) + new


def _ssd_fwd_kernel(xs_ref, bm_ref, cm_ref, dt_ref, alog_ref, dsk_ref, y_ref, st_ref):
    @pl.when(pl.program_id(1) == 0)
    def _():
        st_ref[...] = jnp.zeros_like(st_ref)

    def emit(g, y):
        y_ref[:, g * SSD_GROUP_WIDTH:(g + 1) * SSD_GROUP_WIDTH] = y

    _ssd_chunk(0, xs_ref, bm_ref, cm_ref, dt_ref, alog_ref, st_ref, emit, skip_ref=dsk_ref)


def _ssd_bwd_kernel(xs_ref, bm_ref, cm_ref, dt_ref, alog_ref, yf_ref, z_ref, mod_ref, nw_ref, wo_ref,
                    o_ref, st_ref, u_ref):
    @pl.when(pl.program_id(1) == 0)
    def _():
        st_ref[...] = jnp.zeros_like(st_ref)

    def emit(g, y):
        sl = slice(g * SSD_GROUP_WIDTH, (g + 1) * SSD_GROUP_WIDTH)
        u = (y + yf_ref[:, sl]) * _silu(z_ref[:, sl].astype(F32))
        ms = jnp.mean(u * u, axis=-1, keepdims=True)
        u_ref[:, sl] = (u * lax.rsqrt(ms + EPS) * nw_ref[:, sl]).astype(BF16)

    _ssd_chunk(1, xs_ref, bm_ref, cm_ref, dt_ref, alog_ref, st_ref, emit)
    y = jnp.dot(u_ref[...], wo_ref[...], preferred_element_type=F32)
    o_ref[...] = (mod_ref[2:3, :] * y).astype(BF16)


def _ssd_scans(mods, layer, z, xs, bm, cm, dt, a_log, d_skip, norm_w, w_out):
    bsz, t, width = xs.shape
    d = w_out.shape[1]
    gs_width = bm.shape[-1]
    q = SSD_CHUNK
    n_chunks = t // q
    ctx_chunks = CTX_LEN // q
    fwd = lambda b, i: (b, i, 0)
    bwd_chunk = lambda i: jnp.where(i < ctx_chunks, ctx_chunks - 1 - i, n_chunks - 1 + ctx_chunks - i)
    bwd = lambda b, i: (b, bwd_chunk(i), 0)
    const = lambda a: pl.BlockSpec(a.shape, lambda b, i: (0,) * a.ndim)
    state = pltpu.VMEM((SSD_GROUPS, SSD_STATE, SSD_GROUP_WIDTH), F32)
    chunk_specs = lambda imap: [
        pl.BlockSpec((None, q, width), imap), pl.BlockSpec((None, q, gs_width), imap),
        pl.BlockSpec((None, q, gs_width), imap), pl.BlockSpec((None, q, LANES), imap)]

    y_f = pl.pallas_call(
        _ssd_fwd_kernel,
        grid=(bsz, n_chunks),
        in_specs=[*chunk_specs(fwd), const(a_log), const(d_skip)],
        out_specs=pl.BlockSpec((None, q, width), fwd),
        out_shape=jax.ShapeDtypeStruct((bsz, t, width), F32),
        scratch_shapes=[state],
        compiler_params=pltpu.CompilerParams(
            dimension_semantics=("parallel", "arbitrary"), vmem_limit_bytes=VMEM_LIMIT),
        name="ssd_fwd",
    )(xs, bm, cm, dt, a_log, d_skip)

    return pl.pallas_call(
        _ssd_bwd_kernel,
        grid=(bsz, n_chunks),
        in_specs=[
            *chunk_specs(bwd), const(a_log),
            pl.BlockSpec((None, q, width), bwd),
            pl.BlockSpec((None, q, width), bwd),
            pl.BlockSpec((None, None, 3, d),
                         lambda b, i: (layer, _mod_row(bwd_chunk(i) < ctx_chunks, b), 0, 0)),
            const(norm_w), const(w_out),
        ],
        out_specs=pl.BlockSpec((None, q, d), bwd),
        out_shape=jax.ShapeDtypeStruct((bsz, t, d), BF16),
        scratch_shapes=[state, pltpu.VMEM((q, width), BF16)],
        compiler_params=pltpu.CompilerParams(
            dimension_semantics=("parallel", "arbitrary"), vmem_limit_bytes=VMEM_LIMIT),
        name="ssd_bwd",
    )(xs, bm, cm, dt, a_log, y_f, z, mods, norm_w, w_out)


def _final_norm_kernel(h_ref, delta_ref, w_ref, o_ref):
    x = h_ref[...] + delta_ref[...].astype(F32)
    ms = jnp.mean(x * x, axis=-1, keepdims=True)
    o_ref[...] = x * lax.rsqrt(ms + EPS) * w_ref[...]


def _final_norm(hs, pending, w):
    bsz, t, d = hs.shape
    ctx_tiles = CTX_LEN // ROW_TILE
    n_lat = t - CTX_LEN
    lat_rows = lambda width: pl.BlockSpec((None, ROW_TILE, width), lambda b, i: (b, i + ctx_tiles, 0))
    return pl.pallas_call(
        _final_norm_kernel,
        grid=(bsz, n_lat // ROW_TILE),
        in_specs=[lat_rows(d), lat_rows(d), pl.BlockSpec((1, d), lambda b, i: (0, 0))],
        out_specs=pl.BlockSpec((None, ROW_TILE, d), lambda b, i: (b, i, 0)),
        out_shape=jax.ShapeDtypeStruct((bsz, n_lat, d), F32),
        compiler_params=pltpu.CompilerParams(
            dimension_semantics=("parallel", "parallel"), vmem_limit_bytes=VMEM_LIMIT),
        name="final_norm",
    )(hs, pending, w)


def _rope_tables(seq_len):
    quarter = HEAD_DIM // 4
    pos = jnp.arange(seq_len, dtype=jnp.int32)
    row = (pos // GRID_W).astype(F32)
    col = (pos % GRID_W).astype(F32)
    inv_freq = ROPE_BASE ** (-jnp.arange(0, 2 * quarter, 2, dtype=F32) / (2 * quarter))
    lane = jnp.arange(LANES)
    in_head = lane % HEAD_DIM
    p = jnp.where((in_head < HEAD_DIM // 2)[None, :], row[:, None], col[:, None])
    ang = p * inv_freq[lane % quarter][None, :]
    first = ((lane % (2 * quarter)) < quarter)[None, :]
    cos = jnp.cos(ang)
    sin = jnp.sin(ang)
    sa = jnp.where(first, -sin, 0.0)
    sb = jnp.where(first, 0.0, sin)
    ident = jnp.ones((CTX_LEN, LANES), F32)
    zeros = jnp.zeros((CTX_LEN, LANES), F32)
    return (jnp.concatenate([ident, cos], 0), jnp.concatenate([zeros, sa], 0),
            jnp.concatenate([zeros, sb], 0))


def kernel(x, c, ctx, c_ctx, w_ada, b_ada, attn_w_in, attn_sink, attn_w_out, ssd_w_in, ssd_conv_w,
           ssd_conv_b, ssd_dt_bias, ssd_a_log, ssd_d, ssd_norm_w, ssd_w_out, final_norm_w):
    bsz, seq_len, d = x.shape
    depth = w_ada.shape[0]
    assert ctx.shape[1] == CTX_LEN and CTX_LEN % ROW_TILE == 0 and seq_len % ROW_TILE == 0
    assert bsz <= 8

    cond = jnp.zeros((16, d), F32).at[:bsz].set(c).at[8].set(c_ctx)
    mods = _modulation_tables(cond, w_ada, b_ada).reshape(depth, 16, 3, d)
    cos, sa, sb = _rope_tables(seq_len)
    hs = jnp.concatenate([ctx, x], axis=1)

    ssd_width = ssd_w_out.shape[1]
    conv_dim = ssd_conv_w.shape[-1]
    pad_lanes = lambda a: jnp.pad(a, ((0, 0), (0, LANES - a.shape[-1])))

    assert depth % 2 == 0
    pending = None
    for i in range(depth):
        j = i // 2
        if i % 2 == 0:
            hs, q, k, v, g = _attn_in(hs, pending, mods, i, attn_w_in[j].astype(BF16), cos, sa, sb)
            pending = _attn_core(mods, i, q, k, v, g, attn_sink[j], attn_w_out[j].astype(BF16))
        else:
            w_in = ssd_w_in[j]
            w_z = w_in[:, :ssd_width].astype(BF16)
            w_xbc = w_in[:, ssd_width:ssd_width + conv_dim].astype(BF16)
            w_dt = pad_lanes(w_in[:, ssd_width + conv_dim:]).astype(BF16)
            hs, z, xs, bm, cm, dt = _ssd_in(
                hs, pending, mods, i, w_z, w_xbc, w_dt, ssd_conv_w[j], ssd_conv_b[j][None, :],
                pad_lanes(ssd_dt_bias[j].reshape(1, -1)))
            pending = _ssd_scans(
                mods, i, z, xs, bm, cm, dt, pad_lanes(ssd_a_log[j].reshape(1, -1)),
                jnp.repeat(ssd_d[j], HEAD_DIM)[None, :], ssd_norm_w[j][None, :],
                ssd_w_out[j].astype(BF16))
    return _final_norm(hs, pending, final_norm_w[None, :])
```

```python
import functools
import math

import jax
import jax.numpy as jnp
from jax import lax
from jax.experimental import pallas as pl
from jax.experimental.pallas import tpu as pltpu

F32 = jnp.float32
BF16 = jnp.bfloat16

EPS = 1e-6
CTX_LEN = 256
GRID_W = 64
ROPE_BASE = 10000.0

HEAD_DIM = 64
ATTN_HEADS = 16
ATTN_KV_HEADS = 4
ATTN_GROUP = ATTN_HEADS // ATTN_KV_HEADS
ATTN_BLOCK = 128
LANES = 128
SUBLANES = 8

SSD_HEADS = 32
SSD_GROUPS = 8
SSD_HEADS_PER_GROUP = SSD_HEADS // SSD_GROUPS
SSD_STATE = 128
SSD_CONV = 5
SSD_CHUNK = 128
SSD_GROUP_WIDTH = SSD_HEADS_PER_GROUP * HEAD_DIM

ROW_TILE = 256
CONV_COL_TILE = 512
MASKED = -1e30
VMEM_LIMIT = 56 * 1024 * 1024


def _silu(x):
    return x / (1.0 + jnp.exp2(x * -math.log2(math.e)))


def _rms_normalize(x):
    ms = jnp.mean(x * x, axis=-1, keepdims=True)
    return x * lax.rsqrt(ms + EPS)


def _norm_modulate(x, mod_ref):
    return _rms_normalize(x) * (1.0 + mod_ref[1:2, :]) + mod_ref[0:1, :]


def _mod_row(is_ctx, b):
    return jnp.where(is_ctx, 8, b)


def _mod_kernel(cond_ref, w_ref, b_ref, o_ref):
    cnd = cond_ref[...]
    o_ref[...] = jnp.dot(_silu(cnd), w_ref[...], preferred_element_type=F32,
                         precision=lax.Precision.HIGHEST) + b_ref[...]


def _modulation_tables(cond, w_ada, b_ada):
    depth, d, d3 = w_ada.shape
    return pl.pallas_call(
        _mod_kernel,
        grid=(depth, d3 // d),
        in_specs=[
            pl.BlockSpec((16, d), lambda l, j: (0, 0)),
            pl.BlockSpec((None, d, d), lambda l, j: (l, 0, j)),
            pl.BlockSpec((None, 1, d), lambda l, j: (l, 0, j)),
        ],
        out_specs=pl.BlockSpec((None, 16, d), lambda l, j: (l, 0, j)),
        out_shape=jax.ShapeDtypeStruct((depth, 16, d3), F32),
        compiler_params=pltpu.CompilerParams(vmem_limit_bytes=VMEM_LIMIT),
        name="adaln_tables",
    )(cond, w_ada, b_ada.reshape(depth, 1, d3))


def _attn_in_kernel(h_ref, mod_ref, w_ref, cos_ref, sa_ref, sb_ref, q_ref, k_ref, v_ref, g_ref):
    d = h_ref.shape[-1]
    kvw = ATTN_KV_HEADS * HEAD_DIM
    ub = _norm_modulate(h_ref[...], mod_ref).astype(BF16)
    cos = cos_ref[...]
    sa = sa_ref[...]
    sb = sb_ref[...]

    def rope(t):
        return t * cos + pltpu.roll(t, LANES - 16, 1) * sa + pltpu.roll(t, 16, 1) * sb

    left = lax.broadcasted_iota(jnp.int32, (h_ref.shape[0], LANES), 1) < HEAD_DIM

    def store_doubled(ref, j, t):
        sw = pltpu.roll(t, HEAD_DIM, 1)
        ref[:, (2 * j) * LANES:(2 * j + 1) * LANES] = jnp.where(left, t, sw).astype(BF16)
        ref[:, (2 * j + 1) * LANES:(2 * j + 2) * LANES] = jnp.where(left, sw, t).astype(BF16)

    q = jnp.dot(ub, w_ref[:, 0:d], preferred_element_type=F32)
    scale = HEAD_DIM ** -0.5 * math.log2(math.e)
    for j in range(d // LANES):
        sl = slice(j * LANES, (j + 1) * LANES)
        q_ref[:, sl] = (rope(q[:, sl]) * scale).astype(BF16)
    k = jnp.dot(ub, w_ref[:, d:d + kvw], preferred_element_type=F32)
    v = jnp.dot(ub, w_ref[:, d + kvw:d + 2 * kvw], preferred_element_type=F32)
    for j in range(kvw // LANES):
        sl = slice(j * LANES, (j + 1) * LANES)
        store_doubled(k_ref, j, rope(k[:, sl]))
        store_doubled(v_ref, j, v[:, sl])
    g = jnp.dot(ub, w_ref[:, d + 2 * kvw:], preferred_element_type=F32)
    g_ref[...] = _silu(g).astype(BF16)


def _attn_in(hs, mods, layer, w_in, cos, sa, sb):
    bsz, t, d = hs.shape
    kvw = ATTN_KV_HEADS * HEAD_DIM
    n_in = w_in.shape[1]
    ctx_tiles = CTX_LEN // ROW_TILE
    row_spec = lambda width: pl.BlockSpec((None, ROW_TILE, width), lambda b, i: (b, i, 0))
    tab_spec = pl.BlockSpec((ROW_TILE, LANES), lambda b, i: (i, 0))
    return pl.pallas_call(
        _attn_in_kernel,
        grid=(bsz, t // ROW_TILE),
        in_specs=[
            row_spec(d),
            pl.BlockSpec((None, None, 3, d), lambda b, i: (layer, _mod_row(i < ctx_tiles, b), 0, 0)),
            pl.BlockSpec((d, n_in), lambda b, i: (0, 0)),
            tab_spec, tab_spec, tab_spec,
        ],
        out_specs=[row_spec(d), row_spec(2 * kvw), row_spec(2 * kvw), row_spec(d)],
        out_shape=[
            jax.ShapeDtypeStruct((bsz, t, d), BF16),
            jax.ShapeDtypeStruct((bsz, t, 2 * kvw), BF16),
            jax.ShapeDtypeStruct((bsz, t, 2 * kvw), BF16),
            jax.ShapeDtypeStruct((bsz, t, d), BF16),
        ],
        compiler_params=pltpu.CompilerParams(
            dimension_semantics=("parallel", "parallel"), vmem_limit_bytes=VMEM_LIMIT),
        name="attn_in",
    )(hs, mods, w_in, cos, sa, sb)


def _attn_core_kernel(n_lat_blocks, sink_ref, q_ref, kp_ref, kc_ref, kn_ref, kx_ref,
                      vp_ref, vc_ref, vn_ref, vx_ref, g_ref, h_ref, mod_ref, wo_ref, o_ref,
                      kcat, vdiag):
    blk = pl.program_id(1)
    ctx_blocks = CTX_LEN // ATTN_BLOCK
    j = blk - ctx_blocks
    nb = ATTN_BLOCK
    pairs = ATTN_GROUP // 2
    log2e = math.log2(math.e)
    nt_dims = (((1,), (1,)), ((), ()))

    def left(rows):
        return lax.broadcasted_iota(jnp.int32, (rows, LANES), 1) < HEAD_DIM

    def split_heads(t):
        zero = jnp.zeros_like(t)
        return jnp.where(left(t.shape[0]), t, zero), jnp.where(left(t.shape[0]), zero, t)

    def stacked_queries(kh):
        rows = []
        for pr in range(pairs):
            rows.extend(split_heads(q_ref[:, (kh * pairs + pr) * LANES:(kh * pairs + pr + 1) * LANES]))
        return jnp.concatenate(rows, axis=0)

    def softmax(kh, s):
        sink = jnp.concatenate(
            [jnp.full((nb, 1), sink_ref[kh * ATTN_GROUP + r] * log2e, F32) for r in range(ATTN_GROUP)],
            axis=0)
        m = jnp.maximum(jnp.max(s, axis=-1, keepdims=True), sink)
        p = jnp.exp2(s - m)
        den = jnp.sum(p, axis=-1, keepdims=True) + jnp.exp2(sink - m)
        return p.astype(BF16), 1.0 / den

    def weighted_values(kh, pb, r, n_keys):
        lhs = jnp.concatenate(
            [jnp.concatenate([pb[2 * pr * nb:(2 * pr + 1) * nb], pb[(2 * pr + 1) * nb:(2 * pr + 2) * nb]],
                             axis=1) for pr in range(pairs)], axis=0)
        o = jnp.dot(lhs, vdiag[kh, 0:2 * n_keys, :], preferred_element_type=F32)
        gated = []
        for pr in range(pairs):
            lo, mid, hi = 2 * pr * nb, (2 * pr + 1) * nb, (2 * pr + 2) * nb
            rr = jnp.where(left(nb), r[lo:mid], r[mid:hi])
            sl = slice((kh * pairs + pr) * LANES, (kh * pairs + pr + 1) * LANES)
            gated.append((o[pr * nb:(pr + 1) * nb] * rr * g_ref[:, sl].astype(F32)).astype(BF16))
        rows = slice(kh * pairs * LANES, (kh + 1) * pairs * LANES)
        return jnp.dot(jnp.concatenate(gated, axis=1), wo_ref[rows, :], preferred_element_type=F32)

    def attend(key_blocks, bias):
        n_keys = sum(k_ref.shape[0] for k_ref, _ in key_blocks)
        r0 = 0
        for k_ref, v_ref in key_blocks:
            r1 = r0 + k_ref.shape[0]
            kcat[r0:r1, :] = k_ref[...]
            for kh in range(ATTN_KV_HEADS):
                top, bot = split_heads(v_ref[:, kh * LANES:(kh + 1) * LANES])
                vdiag[kh, r0:r1, :] = top
                vdiag[kh, n_keys + r0:n_keys + r1, :] = bot
            r0 = r1

        def logits(kh):
            s = lax.dot_general(stacked_queries(kh), kcat[0:n_keys, kh * LANES:(kh + 1) * LANES],
                                nt_dims, preferred_element_type=F32)
            if not bias:
                return s
            return jnp.concatenate(
                [s[:, i * nb:(i + 1) * nb] + bias[i] if i in bias else s[:, i * nb:(i + 1) * nb]
                 for i in range(n_keys // nb)], axis=1)

        s_next = logits(0)
        pending = None
        y = None
        for kh in range(ATTN_KV_HEADS):
            s = s_next
            if kh + 1 < ATTN_KV_HEADS:
                s_next = logits(kh + 1)
            pb, r = softmax(kh, s)
            if pending is not None:
                part = weighted_values(*pending, n_keys)
                y = part if y is None else y + part
            pending = (kh, pb, r)
        y = y + weighted_values(*pending, n_keys)
        o_ref[...] = h_ref[...] + mod_ref[2:3, :] * y

    @pl.when(blk >= ctx_blocks)
    def _():
        row = lax.broadcasted_iota(jnp.int32, (nb, nb), 0)
        col = lax.broadcasted_iota(jnp.int32, (nb, nb), 1)
        bias_prev = jnp.where((col >= row) & (j > 0), 0.0, MASKED).astype(F32)
        bias_next = jnp.where((col <= row) & (j < n_lat_blocks - 1), 0.0, MASKED).astype(F32)
        attend(((kp_ref, vp_ref), (kc_ref, vc_ref), (kn_ref, vn_ref), (kx_ref, vx_ref)),
               {0: jnp.concatenate([bias_prev] * ATTN_GROUP, axis=0),
                2: jnp.concatenate([bias_next] * ATTN_GROUP, axis=0)})

    @pl.when(blk < ctx_blocks)
    def _():
        attend(((kx_ref, vx_ref),), {})


def _attn_core(hs, mods, layer, q, k, v, g, sink, w_out):
    bsz, t, d = hs.shape
    kvw = ATTN_KV_HEADS * LANES
    nb = ATTN_BLOCK
    ctx_blocks = CTX_LEN // nb
    n_blocks = t // nb
    n_lat = n_blocks - ctx_blocks
    blk_spec = lambda width, imap: pl.BlockSpec((None, nb, width), imap)
    own = lambda b, i: (b, i, 0)
    prev = lambda b, i: (b, jnp.maximum(i - 1, ctx_blocks), 0)
    nxt = lambda b, i: (b, jnp.minimum(jnp.maximum(i + 1, ctx_blocks), n_blocks - 1), 0)
    ctx_spec = pl.BlockSpec((None, CTX_LEN, kvw), lambda b, i: (b, 0, 0))
    kv_specs = [blk_spec(kvw, prev), blk_spec(kvw, own), blk_spec(kvw, nxt), ctx_spec]
    return pl.pallas_call(
        functools.partial(_attn_core_kernel, n_lat),
        grid=(bsz, n_blocks),
        in_specs=[
            pl.BlockSpec(memory_space=pltpu.SMEM),
            blk_spec(d, own),
            *kv_specs, *kv_specs,
            blk_spec(d, own),
            blk_spec(d, own),
            pl.BlockSpec((None, None, 3, d), lambda b, i: (layer, _mod_row(i < ctx_blocks, b), 0, 0)),
            pl.BlockSpec((d, d), lambda b, i: (0, 0)),
        ],
        out_specs=blk_spec(d, own),
        out_shape=jax.ShapeDtypeStruct((bsz, t, d), F32),
        scratch_shapes=[
            pltpu.VMEM((3 * nb + CTX_LEN, kvw), BF16),
            pltpu.VMEM((ATTN_KV_HEADS, 2 * (3 * nb + CTX_LEN), LANES), BF16),
        ],
        compiler_params=pltpu.CompilerParams(
            dimension_semantics=("parallel", "parallel"), vmem_limit_bytes=VMEM_LIMIT),
        name="attn_core",
    )(sink, q, k, k, k, k, v, v, v, v, g, hs, mods, w_out)


def _ssd_in_kernel(h_ref, hp_ref, hn_ref, mod_ref, wz_ref, wx_ref, wdt_ref, cw_ref, cb_ref, dtb_ref,
                   z_ref, xs_ref, bm_ref, cm_ref, dt_ref):
    blk = pl.program_id(1)
    n_tiles = pl.num_programs(1)
    ctx_tiles = CTX_LEN // ROW_TILE
    halo = SUBLANES
    tm = ROW_TILE
    width = xs_ref.shape[-1]
    gs_width = bm_ref.shape[-1]

    prev_ok = (blk > ctx_tiles) | ((blk > 0) & (blk < ctx_tiles))
    next_ok = ((blk >= ctx_tiles) & (blk < n_tiles - 1)) | (blk < ctx_tiles - 1)
    u_main = _norm_modulate(h_ref[...], mod_ref)
    u_all = jnp.concatenate(
        [_norm_modulate(hp_ref[...], mod_ref) * prev_ok.astype(F32), u_main,
         _norm_modulate(hn_ref[...], mod_ref) * next_ok.astype(F32)], axis=0).astype(BF16)
    ub = u_main.astype(BF16)

    z_ref[...] = jnp.dot(ub, wz_ref[...], preferred_element_type=F32).astype(BF16)
    dt_raw = jnp.dot(ub, wdt_ref[...], preferred_element_type=F32) + dtb_ref[...]
    dt_ref[...] = jnp.maximum(dt_raw, 0.0) + jnp.log1p(jnp.exp(-jnp.abs(dt_raw)))

    pad = SSD_CONV // 2
    n_rows = tm + 2 * halo
    sub = lax.broadcasted_iota(jnp.int32, (1, SUBLANES, CONV_COL_TILE), 1)
    for c in range(wx_ref.shape[-1] // CONV_COL_TILE):
        cs = slice(c * CONV_COL_TILE, (c + 1) * CONV_COL_TILE)
        pre = jnp.dot(u_all, wx_ref[:, cs], preferred_element_type=F32)
        tiles = pre.reshape(n_rows // SUBLANES, SUBLANES, CONV_COL_TILE)
        n_out = tm // SUBLANES
        acc = cb_ref[:, cs] + cw_ref[pad:pad + 1, cs] * tiles[1:1 + n_out]
        for kk in range(SSD_CONV):
            off = kk - pad
            if off == 0:
                continue
            rot = pltpu.roll(tiles, (-off) % SUBLANES, 1)
            same = (sub + off < SUBLANES) if off > 0 else (sub + off >= 0)
            nbr = 1 + (1 if off > 0 else -1)
            acc = acc + cw_ref[kk:kk + 1, cs] * jnp.where(same, rot[1:1 + n_out], rot[nbr:nbr + n_out])
        out = _silu(acc).reshape(tm, CONV_COL_TILE).astype(BF16)
        lo = c * CONV_COL_TILE
        if lo < width:
            xs_ref[:, lo:lo + CONV_COL_TILE] = out
        elif lo < width + gs_width:
            bm_ref[:, lo - width:lo - width + CONV_COL_TILE] = out
        else:
            cm_ref[:, lo - width - gs_width:lo - width - gs_width + CONV_COL_TILE] = out


def _ssd_in(hs, mods, layer, w_z, w_xbc, w_dt, conv_w, conv_b, dt_bias):
    bsz, t, d = hs.shape
    width = w_z.shape[1]
    gs_width = SSD_GROUPS * SSD_STATE
    n_tiles = t // ROW_TILE
    ctx_tiles = CTX_LEN // ROW_TILE
    per_tile = ROW_TILE // SUBLANES
    n_halo_blocks = t // SUBLANES
    row_spec = lambda w_: pl.BlockSpec((None, ROW_TILE, w_), lambda b, i: (b, i, 0))
    full = lambda a: pl.BlockSpec(a.shape, lambda b, i: (0,) * a.ndim)
    return pl.pallas_call(
        _ssd_in_kernel,
        grid=(bsz, n_tiles),
        in_specs=[
            row_spec(d),
            pl.BlockSpec((None, SUBLANES, d), lambda b, i: (b, jnp.maximum(i * per_tile - 1, 0), 0)),
            pl.BlockSpec((None, SUBLANES, d),
                         lambda b, i: (b, jnp.minimum((i + 1) * per_tile, n_halo_blocks - 1), 0)),
            pl.BlockSpec((None, None, 3, d), lambda b, i: (layer, _mod_row(i < ctx_tiles, b), 0, 0)),
            full(w_z), full(w_xbc), full(w_dt), full(conv_w), full(conv_b), full(dt_bias),
        ],
        out_specs=[row_spec(width), row_spec(width), row_spec(gs_width), row_spec(gs_width),
                   row_spec(LANES)],
        out_shape=[
            jax.ShapeDtypeStruct((bsz, t, width), BF16),
            jax.ShapeDtypeStruct((bsz, t, width), BF16),
            jax.ShapeDtypeStruct((bsz, t, gs_width), BF16),
            jax.ShapeDtypeStruct((bsz, t, gs_width), BF16),
            jax.ShapeDtypeStruct((bsz, t, LANES), F32),
        ],
        compiler_params=pltpu.CompilerParams(
            dimension_semantics=("parallel", "parallel"), vmem_limit_bytes=VMEM_LIMIT),
        name="ssd_in",
    )(hs, hs, hs, mods, w_z, w_xbc, w_dt, conv_w, conv_b, dt_bias)


def _ssd_chunk(direction, xs_ref, bm_ref, cm_ref, dt_ref, alog_ref, st_ref, emit, skip_ref=None,
               issue_ahead=False):
    q = SSD_CHUNK
    log2e = math.log2(math.e)
    dt = dt_ref[...]
    da = dt * (-jnp.exp(alog_ref[...]))
    row = lax.broadcasted_iota(jnp.int32, (q, q), 0)
    col = lax.broadcasted_iota(jnp.int32, (q, q), 1)
    seen = (row >= col) if direction == 0 else (col >= row)
    a = jnp.dot(seen.astype(F32), da, preferred_element_type=F32,
                precision=lax.Precision.HIGHEST)
    last = q - 1 if direction == 0 else 0
    a_tot = a[last:last + 1, :]
    w_end = jnp.exp(a_tot - a) * dt
    dec = jnp.exp(a_tot)
    a2 = a * log2e
    src2_t = ((a - jnp.log(dt)) * log2e).T
    left = col < HEAD_DIM

    def group_products(g):
        gsl = slice(g * SSD_STATE, (g + 1) * SSD_STATE)
        cmg = cm_ref[:, gsl]
        bmg = bm_ref[:, gsl]
        cb = lax.dot_general(cmg, bmg, (((1,), (1,)), ((), ())), preferred_element_type=F32)
        h_in = st_ref[g]
        y_off = jnp.dot(cmg, h_in.astype(BF16), preferred_element_type=F32)
        return bmg, cb, h_in, y_off

    ahead = group_products(0) if issue_ahead else None
    for g in range(SSD_GROUPS):
        bmg, cb, h_in, y_off = ahead if issue_ahead else group_products(g)
        if issue_ahead and g + 1 < SSD_GROUPS:
            ahead = group_products(g + 1)
        ys, xws, decs = [], [], []
        for pr in range(SSD_HEADS_PER_GROUP // 2):
            pair = g * (SSD_HEADS_PER_GROUP // 2) + pr
            c0 = direction * SSD_HEADS + 2 * pair
            psl = slice(pair * LANES, (pair + 1) * LANES)
            xp = xs_ref[:, psl]
            ws, eas = [], []
            for c in (c0, c0 + 1):
                a_col = jnp.broadcast_to(a2[:, c:c + 1], (q, q))
                lmat = jnp.exp2(jnp.where(seen, a_col - src2_t[c:c + 1, :], -jnp.inf))
                ws.append((cb * lmat).astype(BF16))
                eas.append(jnp.exp2(a_col))
            zero = jnp.zeros_like(xp)
            x_diag = jnp.concatenate([jnp.where(left, xp, zero), jnp.where(left, zero, xp)], axis=0)
            y_diag = jnp.dot(jnp.concatenate(ws, axis=1), x_diag, preferred_element_type=F32)
            y = y_diag + y_off[:, pr * LANES:(pr + 1) * LANES] * jnp.where(left, eas[0], eas[1])
            xf = xp.astype(F32)
            if skip_ref is not None:
                y = y + skip_ref[:, psl] * xf
            ys.append(y)
            idx = jnp.where(left, c0, c0 + 1)
            xws.append((xf * jnp.take_along_axis(w_end, idx, axis=1)).astype(BF16))
            decs.append(jnp.where(left[0:1, :], jnp.broadcast_to(dec[:, c0:c0 + 1], (1, LANES)),
                                  jnp.broadcast_to(dec[:, c0 + 1:c0 + 2], (1, LANES))))
        emit(g, jnp.concatenate(ys, axis=1))
        new = lax.dot_general(bmg, jnp.concatenate(xws, axis=1), (((0,), (0,)), ((), ())),
                              preferred_element_type=F32)
        st_ref[g] = h_in * jnp.concatenate(decs, axis=1) + new


def _ssd_fwd_kernel(xs_ref, bm_ref, cm_ref, dt_ref, alog_ref, dsk_ref, y_ref, st_ref):
    @pl.when(pl.program_id(1) == 0)
    def _():
        st_ref[...] = jnp.zeros_like(st_ref)

    def emit(g, y):
        y_ref[:, g * SSD_GROUP_WIDTH:(g + 1) * SSD_GROUP_WIDTH] = y

    _ssd_chunk(0, xs_ref, bm_ref, cm_ref, dt_ref, alog_ref, st_ref, emit, skip_ref=dsk_ref,
               issue_ahead=True)


def _ssd_bwd_kernel(final, xs_ref, bm_ref, cm_ref, dt_ref, alog_ref, yf_ref, z_ref, h_ref, mod_ref,
                    nw_ref, wo_ref, fw_ref, o_ref, st_ref, u_ref):
    @pl.when(pl.program_id(1) == 0)
    def _():
        st_ref[...] = jnp.zeros_like(st_ref)

    def emit(g, y):
        sl = slice(g * SSD_GROUP_WIDTH, (g + 1) * SSD_GROUP_WIDTH)
        u = (y + yf_ref[:, sl]) * _silu(z_ref[:, sl].astype(F32))
        u_ref[:, sl] = (_rms_normalize(u) * nw_ref[:, sl]).astype(BF16)

    _ssd_chunk(1, xs_ref, bm_ref, cm_ref, dt_ref, alog_ref, st_ref, emit)
    y = jnp.dot(u_ref[...], wo_ref[...], preferred_element_type=F32)
    h = h_ref[...] + mod_ref[2:3, :] * y
    o_ref[...] = _rms_normalize(h) * fw_ref[...] if final else h


def _ssd_scans(hs, mods, layer, z, xs, bm, cm, dt, a_log, d_skip, norm_w, w_out, final_w):
    bsz, t, d = hs.shape
    width = xs.shape[-1]
    gs_width = bm.shape[-1]
    q = SSD_CHUNK
    n_chunks = t // q
    ctx_chunks = CTX_LEN // q
    fwd = lambda b, i: (b, i, 0)
    bwd_chunk = lambda i: jnp.where(i < ctx_chunks, ctx_chunks - 1 - i, n_chunks - 1 + ctx_chunks - i)
    bwd = lambda b, i: (b, bwd_chunk(i), 0)
    const = lambda a: pl.BlockSpec(a.shape, lambda b, i: (0,) * a.ndim)
    state = pltpu.VMEM((SSD_GROUPS, SSD_STATE, SSD_GROUP_WIDTH), F32)
    chunk_specs = lambda imap: [
        pl.BlockSpec((None, q, width), imap), pl.BlockSpec((None, q, gs_width), imap),
        pl.BlockSpec((None, q, gs_width), imap), pl.BlockSpec((None, q, LANES), imap)]

    y_f = pl.pallas_call(
        _ssd_fwd_kernel,
        grid=(bsz, n_chunks),
        in_specs=[*chunk_specs(fwd), const(a_log), const(d_skip)],
        out_specs=pl.BlockSpec((None, q, width), fwd),
        out_shape=jax.ShapeDtypeStruct((bsz, t, width), F32),
        scratch_shapes=[state],
        compiler_params=pltpu.CompilerParams(
            dimension_semantics=("parallel", "arbitrary"), vmem_limit_bytes=VMEM_LIMIT),
        name="ssd_fwd",
    )(xs, bm, cm, dt, a_log, d_skip)

    final = final_w is not None
    if final:
        out_rows = t - CTX_LEN
        out_map = lambda b, i: (b, bwd_chunk(jnp.maximum(i, ctx_chunks)) - ctx_chunks, 0)
    else:
        final_w = jnp.zeros((1, d), F32)
        out_rows = t
        out_map = bwd
    return pl.pallas_call(
        functools.partial(_ssd_bwd_kernel, final),
        grid=(bsz, n_chunks),
        in_specs=[
            *chunk_specs(bwd), const(a_log),
            pl.BlockSpec((None, q, width), bwd),
            pl.BlockSpec((None, q, width), bwd),
            pl.BlockSpec((None, q, d), bwd),
            pl.BlockSpec((None, None, 3, d),
                         lambda b, i: (layer, _mod_row(bwd_chunk(i) < ctx_chunks, b), 0, 0)),
            const(norm_w), const(w_out), const(final_w),
        ],
        out_specs=pl.BlockSpec((None, q, d), out_map),
        out_shape=jax.ShapeDtypeStruct((bsz, out_rows, d), F32),
        scratch_shapes=[state, pltpu.VMEM((q, width), BF16)],
        compiler_params=pltpu.CompilerParams(
            dimension_semantics=("parallel", "arbitrary"), vmem_limit_bytes=VMEM_LIMIT),
        name="ssd_bwd",
    )(xs, bm, cm, dt, a_log, y_f, z, hs, mods, norm_w, w_out, final_w)


def _rope_tables(seq_len):
    quarter = HEAD_DIM // 4
    pos = jnp.arange(seq_len, dtype=jnp.int32)
    row = (pos // GRID_W).astype(F32)
    col = (pos % GRID_W).astype(F32)
    inv_freq = ROPE_BASE ** (-jnp.arange(0, 2 * quarter, 2, dtype=F32) / (2 * quarter))
    lane = jnp.arange(LANES)
    in_head = lane % HEAD_DIM
    p = jnp.where((in_head < HEAD_DIM // 2)[None, :], row[:, None], col[:, None])
    ang = p * inv_freq[lane % quarter][None, :]
    first = ((lane % (2 * quarter)) < quarter)[None, :]
    cos = jnp.cos(ang)
    sin = jnp.sin(ang)
    sa = jnp.where(first, -sin, 0.0)
    sb = jnp.where(first, 0.0, sin)
    ident = jnp.ones((CTX_LEN, LANES), F32)
    zeros = jnp.zeros((CTX_LEN, LANES), F32)
    return (jnp.concatenate([ident, cos], 0), jnp.concatenate([zeros, sa], 0),
            jnp.concatenate([zeros, sb], 0))


def kernel(x, c, ctx, c_ctx, w_ada, b_ada, attn_w_in, attn_sink, attn_w_out, ssd_w_in, ssd_conv_w,
           ssd_conv_b, ssd_dt_bias, ssd_a_log, ssd_d, ssd_norm_w, ssd_w_out, final_norm_w):
    bsz, seq_len, d = x.shape
    depth = w_ada.shape[0]
    assert ctx.shape[1] == CTX_LEN and CTX_LEN % ROW_TILE == 0 and seq_len % ROW_TILE == 0
    assert bsz <= 8 and depth % 2 == 0

    cond = jnp.zeros((16, d), F32).at[:bsz].set(c).at[8].set(c_ctx)
    mods = _modulation_tables(cond, w_ada, b_ada).reshape(depth, 16, 3, d)
    cos, sa, sb = _rope_tables(seq_len)
    hs = jnp.concatenate([ctx, x], axis=1)

    ssd_width = ssd_w_out.shape[1]
    conv_dim = ssd_conv_w.shape[-1]
    pad_lanes = lambda a: jnp.pad(a, ((0, 0), (0, LANES - a.shape[-1])))

    for i in range(depth):
        j = i // 2
        if i % 2 == 0:
            q, k, v, g = _attn_in(hs, mods, i, attn_w_in[j].astype(BF16), cos, sa, sb)
            hs = _attn_core(hs, mods, i, q, k, v, g, attn_sink[j], attn_w_out[j].astype(BF16))
        else:
            w_in = ssd_w_in[j]
            w_z = w_in[:, :ssd_width].astype(BF16)
            w_xbc = w_in[:, ssd_width:ssd_width + conv_dim].astype(BF16)
            w_dt = pad_lanes(w_in[:, ssd_width + conv_dim:]).astype(BF16)
            z, xs, bm, cm, dt = _ssd_in(
                hs, mods, i, w_z, w_xbc, w_dt, ssd_conv_w[j], ssd_conv_b[j][None, :],
                pad_lanes(ssd_dt_bias[j].reshape(1, -1)))
            hs = _ssd_scans(
                hs, mods, i, z, xs, bm, cm, dt, pad_lanes(ssd_a_log[j].reshape(1, -1)),
                jnp.repeat(ssd_d[j], HEAD_DIM)[None, :], ssd_norm_w[j][None, :],
                ssd_w_out[j].astype(BF16), final_norm_w[None, :] if i == depth - 1 else None)
    return hs
```

```python
import functools
import math

import jax
import jax.numpy as jnp
from jax import lax
from jax.experimental import pallas as pl
from jax.experimental.pallas import tpu as pltpu

F32 = jnp.float32
BF16 = jnp.bfloat16

EPS = 1e-6
CTX_LEN = 256
GRID_W = 64
ROPE_BASE = 10000.0

HEAD_DIM = 64
ATTN_HEADS = 16
ATTN_KV_HEADS = 4
ATTN_GROUP = ATTN_HEADS // ATTN_KV_HEADS
ATTN_BLOCK = 128
LANES = 128
SUBLANES = 8

SSD_HEADS = 32
SSD_GROUPS = 8
SSD_HEADS_PER_GROUP = SSD_HEADS // SSD_GROUPS
SSD_STATE = 128
SSD_CONV = 5
SSD_CHUNK = 128
SSD_GROUP_WIDTH = SSD_HEADS_PER_GROUP * HEAD_DIM

ROW_TILE = 256
CONV_COL_TILE = 512
MASKED = -1e30
VMEM_LIMIT = 56 * 1024 * 1024


def _silu(x):
    return x / (1.0 + jnp.exp2(x * -math.log2(math.e)))


def _rms_normalize(x):
    ms = jnp.mean(x * x, axis=-1, keepdims=True)
    return x * lax.rsqrt(ms + EPS)


def _norm_modulate(x, mod_ref):
    return _rms_normalize(x) * (1.0 + mod_ref[1:2, :]) + mod_ref[0:1, :]


def _mod_row(is_ctx, b):
    return jnp.where(is_ctx, 8, b)


def _mod_kernel(cond_ref, w_ref, b_ref, o_ref):
    cnd = cond_ref[...]
    o_ref[...] = jnp.dot(_silu(cnd), w_ref[...], preferred_element_type=F32,
                         precision=lax.Precision.HIGHEST) + b_ref[...]


def _modulation_tables(cond, w_ada, b_ada):
    depth, d, d3 = w_ada.shape
    return pl.pallas_call(
        _mod_kernel,
        grid=(depth, d3 // d),
        in_specs=[
            pl.BlockSpec((16, d), lambda l, j: (0, 0)),
            pl.BlockSpec((None, d, d), lambda l, j: (l, 0, j)),
            pl.BlockSpec((None, 1, d), lambda l, j: (l, 0, j)),
        ],
        out_specs=pl.BlockSpec((None, 16, d), lambda l, j: (l, 0, j)),
        out_shape=jax.ShapeDtypeStruct((depth, 16, d3), F32),
        compiler_params=pltpu.CompilerParams(vmem_limit_bytes=VMEM_LIMIT),
        name="adaln_tables",
    )(cond, w_ada, b_ada.reshape(depth, 1, d3))


def _attn_in_kernel(split, *refs):
    if split:
        c_ref, h_ref, *refs = refs
        h = jnp.where(pl.program_id(1) < CTX_LEN // ROW_TILE, c_ref[...], h_ref[...])
    else:
        h_ref, *refs = refs
        h = h_ref[...]
    mod_ref, w_ref, cos_ref, sa_ref, sb_ref, q_ref, k_ref, v_ref, g_ref = refs
    d = h_ref.shape[-1]
    kvw = ATTN_KV_HEADS * HEAD_DIM
    ub = _norm_modulate(h, mod_ref).astype(BF16)
    cos = cos_ref[...]
    sa = sa_ref[...]
    sb = sb_ref[...]

    def rope(t):
        return t * cos + pltpu.roll(t, LANES - 16, 1) * sa + pltpu.roll(t, 16, 1) * sb

    left = lax.broadcasted_iota(jnp.int32, (h_ref.shape[0], LANES), 1) < HEAD_DIM

    def store_doubled(ref, j, t):
        sw = pltpu.roll(t, HEAD_DIM, 1)
        ref[:, (2 * j) * LANES:(2 * j + 1) * LANES] = jnp.where(left, t, sw).astype(BF16)
        ref[:, (2 * j + 1) * LANES:(2 * j + 2) * LANES] = jnp.where(left, sw, t).astype(BF16)

    q = jnp.dot(ub, w_ref[:, 0:d], preferred_element_type=F32)
    scale = HEAD_DIM ** -0.5 * math.log2(math.e)
    for j in range(d // LANES):
        sl = slice(j * LANES, (j + 1) * LANES)
        q_ref[:, sl] = (rope(q[:, sl]) * scale).astype(BF16)
    k = jnp.dot(ub, w_ref[:, d:d + kvw], preferred_element_type=F32)
    v = jnp.dot(ub, w_ref[:, d + kvw:d + 2 * kvw], preferred_element_type=F32)
    for j in range(kvw // LANES):
        sl = slice(j * LANES, (j + 1) * LANES)
        store_doubled(k_ref, j, rope(k[:, sl]))
        store_doubled(v_ref, j, v[:, sl])
    g = jnp.dot(ub, w_ref[:, d + 2 * kvw:], preferred_element_type=F32)
    g_ref[...] = _silu(g).astype(BF16)


def _stream_specs(hs, rows):
    if not isinstance(hs, tuple):
        return [pl.BlockSpec((None, rows, hs.shape[-1]), lambda b, i: (b, i, 0))], [hs], hs.shape[1]
    ctx, x = hs
    n_ctx = CTX_LEN // rows
    d = x.shape[-1]
    return ([pl.BlockSpec((None, rows, d), lambda b, i: (b, jnp.minimum(i, n_ctx - 1), 0)),
             pl.BlockSpec((None, rows, d), lambda b, i: (b, jnp.maximum(i - n_ctx, 0), 0))],
            [ctx, x], CTX_LEN + x.shape[1])


def _attn_in(hs, mods, layer, w_in, cos, sa, sb):
    stream_specs, stream, t = _stream_specs(hs, ROW_TILE)
    bsz, _, d = stream[0].shape
    kvw = ATTN_KV_HEADS * HEAD_DIM
    n_in = w_in.shape[1]
    ctx_tiles = CTX_LEN // ROW_TILE
    row_spec = lambda width: pl.BlockSpec((None, ROW_TILE, width), lambda b, i: (b, i, 0))
    tab_spec = pl.BlockSpec((ROW_TILE, LANES), lambda b, i: (i, 0))
    return pl.pallas_call(
        functools.partial(_attn_in_kernel, len(stream) == 2),
        grid=(bsz, t // ROW_TILE),
        in_specs=[
            *stream_specs,
            pl.BlockSpec((None, None, 3, d), lambda b, i: (layer, _mod_row(i < ctx_tiles, b), 0, 0)),
            pl.BlockSpec((d, n_in), lambda b, i: (0, 0)),
            tab_spec, tab_spec, tab_spec,
        ],
        out_specs=[row_spec(d), row_spec(2 * kvw), row_spec(2 * kvw), row_spec(d)],
        out_shape=[
            jax.ShapeDtypeStruct((bsz, t, d), BF16),
            jax.ShapeDtypeStruct((bsz, t, 2 * kvw), BF16),
            jax.ShapeDtypeStruct((bsz, t, 2 * kvw), BF16),
            jax.ShapeDtypeStruct((bsz, t, d), BF16),
        ],
        compiler_params=pltpu.CompilerParams(
            dimension_semantics=("parallel", "parallel"), vmem_limit_bytes=VMEM_LIMIT),
        name="attn_in",
    )(*stream, mods, w_in, cos, sa, sb)


def _attn_core_kernel(n_lat_blocks, split, sink_ref, q_ref, kp_ref, kc_ref, kn_ref, kx_ref,
                      vp_ref, vc_ref, vn_ref, vx_ref, g_ref, *refs):
    if split:
        c_ref, h_ref, mod_ref, wo_ref, o_ref, kcat, vdiag = refs
    else:
        h_ref, mod_ref, wo_ref, o_ref, kcat, vdiag = refs
        c_ref = h_ref
    blk = pl.program_id(1)
    ctx_blocks = CTX_LEN // ATTN_BLOCK
    j = blk - ctx_blocks
    nb = ATTN_BLOCK
    pairs = ATTN_GROUP // 2
    log2e = math.log2(math.e)
    nt_dims = (((1,), (1,)), ((), ()))

    def left(rows):
        return lax.broadcasted_iota(jnp.int32, (rows, LANES), 1) < HEAD_DIM

    def split_heads(t):
        zero = jnp.zeros_like(t)
        return jnp.where(left(t.shape[0]), t, zero), jnp.where(left(t.shape[0]), zero, t)

    def stacked_queries(kh):
        rows = []
        for pr in range(pairs):
            rows.extend(split_heads(q_ref[:, (kh * pairs + pr) * LANES:(kh * pairs + pr + 1) * LANES]))
        return jnp.concatenate(rows, axis=0)

    def softmax(kh, s):
        sink = jnp.concatenate(
            [jnp.full((nb, 1), sink_ref[kh * ATTN_GROUP + r] * log2e, F32) for r in range(ATTN_GROUP)],
            axis=0)
        m = jnp.maximum(jnp.max(s, axis=-1, keepdims=True), sink)
        p = jnp.exp2(s - m)
        den = jnp.sum(p, axis=-1, keepdims=True) + jnp.exp2(sink - m)
        return p.astype(BF16), 1.0 / den

    def weighted_values(kh, pb, r, n_keys):
        lhs = jnp.concatenate(
            [jnp.concatenate([pb[2 * pr * nb:(2 * pr + 1) * nb], pb[(2 * pr + 1) * nb:(2 * pr + 2) * nb]],
                             axis=1) for pr in range(pairs)], axis=0)
        o = jnp.dot(lhs, vdiag[kh, 0:2 * n_keys, :], preferred_element_type=F32)
        gated = []
        for pr in range(pairs):
            lo, mid, hi = 2 * pr * nb, (2 * pr + 1) * nb, (2 * pr + 2) * nb
            rr = jnp.where(left(nb), r[lo:mid], r[mid:hi])
            sl = slice((kh * pairs + pr) * LANES, (kh * pairs + pr + 1) * LANES)
            gated.append((o[pr * nb:(pr + 1) * nb] * rr * g_ref[:, sl].astype(F32)).astype(BF16))
        rows = slice(kh * pairs * LANES, (kh + 1) * pairs * LANES)
        return jnp.dot(jnp.concatenate(gated, axis=1), wo_ref[rows, :], preferred_element_type=F32)

    def attend(key_blocks, bias, res_ref):
        n_keys = sum(k_ref.shape[0] for k_ref, _ in key_blocks)
        r0 = 0
        for k_ref, v_ref in key_blocks:
            r1 = r0 + k_ref.shape[0]
            kcat[r0:r1, :] = k_ref[...]
            for kh in range(ATTN_KV_HEADS):
                top, bot = split_heads(v_ref[:, kh * LANES:(kh + 1) * LANES])
                vdiag[kh, r0:r1, :] = top
                vdiag[kh, n_keys + r0:n_keys + r1, :] = bot
            r0 = r1

        def logits(kh):
            s = lax.dot_general(stacked_queries(kh), kcat[0:n_keys, kh * LANES:(kh + 1) * LANES],
                                nt_dims, preferred_element_type=F32)
            if not bias:
                return s
            return jnp.concatenate(
                [s[:, i * nb:(i + 1) * nb] + bias[i] if i in bias else s[:, i * nb:(i + 1) * nb]
                 for i in range(n_keys // nb)], axis=1)

        s_next = logits(0)
        pending = None
        y = None
        for kh in range(ATTN_KV_HEADS):
            s = s_next
            if kh + 1 < ATTN_KV_HEADS:
                s_next = logits(kh + 1)
            pb, r = softmax(kh, s)
            if pending is not None:
                part = weighted_values(*pending, n_keys)
                y = part if y is None else y + part
            pending = (kh, pb, r)
        y = y + weighted_values(*pending, n_keys)
        o_ref[...] = res_ref[...] + mod_ref[2:3, :] * y

    @pl.when(blk >= ctx_blocks)
    def _():
        row = lax.broadcasted_iota(jnp.int32, (nb, nb), 0)
        col = lax.broadcasted_iota(jnp.int32, (nb, nb), 1)
        bias_prev = jnp.where((col >= row) & (j > 0), 0.0, MASKED).astype(F32)
        bias_next = jnp.where((col <= row) & (j < n_lat_blocks - 1), 0.0, MASKED).astype(F32)
        attend(((kp_ref, vp_ref), (kc_ref, vc_ref), (kn_ref, vn_ref), (kx_ref, vx_ref)),
               {0: jnp.concatenate([bias_prev] * ATTN_GROUP, axis=0),
                2: jnp.concatenate([bias_next] * ATTN_GROUP, axis=0)}, h_ref)

    @pl.when(blk < ctx_blocks)
    def _():
        attend(((kx_ref, vx_ref),), {}, c_ref)


def _attn_core(hs, mods, layer, q, k, v, g, sink, w_out):
    nb = ATTN_BLOCK
    stream_specs, stream, t = _stream_specs(hs, nb)
    bsz, _, d = stream[0].shape
    kvw = ATTN_KV_HEADS * LANES
    ctx_blocks = CTX_LEN // nb
    n_blocks = t // nb
    n_lat = n_blocks - ctx_blocks
    blk_spec = lambda width, imap: pl.BlockSpec((None, nb, width), imap)
    own = lambda b, i: (b, i, 0)
    prev = lambda b, i: (b, jnp.maximum(i - 1, ctx_blocks), 0)
    nxt = lambda b, i: (b, jnp.minimum(jnp.maximum(i + 1, ctx_blocks), n_blocks - 1), 0)
    ctx_spec = pl.BlockSpec((None, CTX_LEN, kvw), lambda b, i: (b, 0, 0))
    kv_specs = [blk_spec(kvw, prev), blk_spec(kvw, own), blk_spec(kvw, nxt), ctx_spec]
    return pl.pallas_call(
        functools.partial(_attn_core_kernel, n_lat, len(stream) == 2),
        grid=(bsz, n_blocks),
        in_specs=[
            pl.BlockSpec(memory_space=pltpu.SMEM),
            blk_spec(d, own),
            *kv_specs, *kv_specs,
            blk_spec(d, own),
            *stream_specs,
            pl.BlockSpec((None, None, 3, d), lambda b, i: (layer, _mod_row(i < ctx_blocks, b), 0, 0)),
            pl.BlockSpec(w_out.shape, lambda b, i: (0, 0)),
        ],
        out_specs=blk_spec(d, own),
        out_shape=jax.ShapeDtypeStruct((bsz, t, d), F32),
        scratch_shapes=[
            pltpu.VMEM((3 * nb + CTX_LEN, kvw), BF16),
            pltpu.VMEM((ATTN_KV_HEADS, 2 * (3 * nb + CTX_LEN), LANES), BF16),
        ],
        compiler_params=pltpu.CompilerParams(
            dimension_semantics=("parallel", "parallel"), vmem_limit_bytes=VMEM_LIMIT),
        name="attn_core",
    )(sink, q, k, k, k, k, v, v, v, v, g, *stream, mods, w_out)


def _ssd_in_kernel(h_ref, hp_ref, hn_ref, mod_ref, w_ref, cw_ref, cb_ref, dtb_ref,
                   z_ref, xs_ref, bm_ref, cm_ref, dt_ref):
    blk = pl.program_id(1)
    n_tiles = pl.num_programs(1)
    ctx_tiles = CTX_LEN // ROW_TILE
    halo = SUBLANES
    tm = ROW_TILE
    width = xs_ref.shape[-1]
    gs_width = bm_ref.shape[-1]

    prev_ok = (blk > ctx_tiles) | ((blk > 0) & (blk < ctx_tiles))
    next_ok = ((blk >= ctx_tiles) & (blk < n_tiles - 1)) | (blk < ctx_tiles - 1)
    u_main = _norm_modulate(h_ref[...], mod_ref)
    u_all = jnp.concatenate(
        [_norm_modulate(hp_ref[...], mod_ref) * prev_ok.astype(F32), u_main,
         _norm_modulate(hn_ref[...], mod_ref) * next_ok.astype(F32)], axis=0).astype(BF16)
    ub = u_main.astype(BF16)

    conv_dim = cw_ref.shape[-1]
    z_ref[...] = jnp.dot(ub, w_ref[:, 0:width], preferred_element_type=F32).astype(BF16)
    dt_raw = jnp.dot(ub, w_ref[:, width + conv_dim:], preferred_element_type=F32) + dtb_ref[...]
    dt_ref[...] = jnp.maximum(dt_raw, 0.0) + jnp.log1p(jnp.exp(-jnp.abs(dt_raw)))

    pad = SSD_CONV // 2
    n_rows = tm + 2 * halo
    sub = lax.broadcasted_iota(jnp.int32, (1, SUBLANES, CONV_COL_TILE), 1)
    for c in range(conv_dim // CONV_COL_TILE):
        cs = slice(c * CONV_COL_TILE, (c + 1) * CONV_COL_TILE)
        pre = jnp.dot(u_all, w_ref[:, width + c * CONV_COL_TILE:width + (c + 1) * CONV_COL_TILE],
                      preferred_element_type=F32)
        tiles = pre.reshape(n_rows // SUBLANES, SUBLANES, CONV_COL_TILE)
        n_out = tm // SUBLANES
        acc = cb_ref[:, cs] + cw_ref[pad:pad + 1, cs] * tiles[1:1 + n_out]
        for kk in range(SSD_CONV):
            off = kk - pad
            if off == 0:
                continue
            rot = pltpu.roll(tiles, (-off) % SUBLANES, 1)
            same = (sub + off < SUBLANES) if off > 0 else (sub + off >= 0)
            nbr = 1 + (1 if off > 0 else -1)
            acc = acc + cw_ref[kk:kk + 1, cs] * jnp.where(same, rot[1:1 + n_out], rot[nbr:nbr + n_out])
        out = _silu(acc).reshape(tm, CONV_COL_TILE).astype(BF16)
        lo = c * CONV_COL_TILE
        if lo < width:
            xs_ref[:, lo:lo + CONV_COL_TILE] = out
        elif lo < width + gs_width:
            bm_ref[:, lo - width:lo - width + CONV_COL_TILE] = out
        else:
            cm_ref[:, lo - width - gs_width:lo - width - gs_width + CONV_COL_TILE] = out


def _ssd_in(hs, mods, layer, w_in, conv_w, conv_b, dt_bias):
    bsz, t, d = hs.shape
    width = w_in.shape[1] - conv_w.shape[1] - LANES
    gs_width = SSD_GROUPS * SSD_STATE
    n_tiles = t // ROW_TILE
    ctx_tiles = CTX_LEN // ROW_TILE
    per_tile = ROW_TILE // SUBLANES
    n_halo_blocks = t // SUBLANES
    row_spec = lambda w_: pl.BlockSpec((None, ROW_TILE, w_), lambda b, i: (b, i, 0))
    full = lambda a: pl.BlockSpec(a.shape, lambda b, i: (0,) * a.ndim)
    return pl.pallas_call(
        _ssd_in_kernel,
        grid=(bsz, n_tiles),
        in_specs=[
            row_spec(d),
            pl.BlockSpec((None, SUBLANES, d), lambda b, i: (b, jnp.maximum(i * per_tile - 1, 0), 0)),
            pl.BlockSpec((None, SUBLANES, d),
                         lambda b, i: (b, jnp.minimum((i + 1) * per_tile, n_halo_blocks - 1), 0)),
            pl.BlockSpec((None, None, 3, d), lambda b, i: (layer, _mod_row(i < ctx_tiles, b), 0, 0)),
            full(w_in), full(conv_w), full(conv_b), full(dt_bias),
        ],
        out_specs=[row_spec(width), row_spec(width), row_spec(gs_width), row_spec(gs_width),
                   row_spec(LANES)],
        out_shape=[
            jax.ShapeDtypeStruct((bsz, t, width), BF16),
            jax.ShapeDtypeStruct((bsz, t, width), BF16),
            jax.ShapeDtypeStruct((bsz, t, gs_width), BF16),
            jax.ShapeDtypeStruct((bsz, t, gs_width), BF16),
            jax.ShapeDtypeStruct((bsz, t, LANES), F32),
        ],
        compiler_params=pltpu.CompilerParams(
            dimension_semantics=("parallel", "parallel"), vmem_limit_bytes=VMEM_LIMIT),
        name="ssd_in",
    )(hs, hs, hs, mods, w_in, conv_w, conv_b, dt_bias)


def _ssd_chunk(direction, xs_ref, bm_ref, cm_ref, dt_ref, alog_ref, st_ref, emit, skip_ref=None,
               issue_ahead=False):
    q = SSD_CHUNK
    log2e = math.log2(math.e)
    dt = dt_ref[...]
    da = dt * (-jnp.exp(alog_ref[...]))
    row = lax.broadcasted_iota(jnp.int32, (q, q), 0)
    col = lax.broadcasted_iota(jnp.int32, (q, q), 1)
    seen = (row >= col) if direction == 0 else (col >= row)
    a = jnp.dot(seen.astype(F32), da, preferred_element_type=F32,
                precision=lax.Precision.HIGHEST)
    last = q - 1 if direction == 0 else 0
    a_tot = a[last:last + 1, :]
    w_end = jnp.exp(a_tot - a) * dt
    dec = jnp.exp(a_tot)
    a2 = a * log2e
    src2_t = ((a - jnp.log(dt)) * log2e).T
    left = col < HEAD_DIM

    def group_products(g):
        gsl = slice(g * SSD_STATE, (g + 1) * SSD_STATE)
        cmg = cm_ref[:, gsl]
        bmg = bm_ref[:, gsl]
        cb = lax.dot_general(cmg, bmg, (((1,), (1,)), ((), ())), preferred_element_type=F32)
        h_in = st_ref[g]
        y_off = jnp.dot(cmg, h_in.astype(BF16), preferred_element_type=F32)
        return bmg, cb, h_in, y_off

    ahead = group_products(0) if issue_ahead else None
    for g in range(SSD_GROUPS):
        bmg, cb, h_in, y_off = ahead if issue_ahead else group_products(g)
        if issue_ahead and g + 1 < SSD_GROUPS:
            ahead = group_products(g + 1)
        ys, xws, decs = [], [], []
        for pr in range(SSD_HEADS_PER_GROUP // 2):
            pair = g * (SSD_HEADS_PER_GROUP // 2) + pr
            c0 = direction * SSD_HEADS + 2 * pair
            psl = slice(pair * LANES, (pair + 1) * LANES)
            xp = xs_ref[:, psl]
            ws, eas = [], []
            for c in (c0, c0 + 1):
                a_col = jnp.broadcast_to(a2[:, c:c + 1], (q, q))
                lmat = jnp.exp2(jnp.where(seen, a_col - src2_t[c:c + 1, :], -jnp.inf))
                ws.append((cb * lmat).astype(BF16))
                eas.append(jnp.exp2(a_col))
            zero = jnp.zeros_like(xp)
            x_diag = jnp.concatenate([jnp.where(left, xp, zero), jnp.where(left, zero, xp)], axis=0)
            y_diag = jnp.dot(jnp.concatenate(ws, axis=1), x_diag, preferred_element_type=F32)
            y = y_diag + y_off[:, pr * LANES:(pr + 1) * LANES] * jnp.where(left, eas[0], eas[1])
            xf = xp.astype(F32)
            if skip_ref is not None:
                y = y + skip_ref[:, psl] * xf
            ys.append(y)
            idx = jnp.where(left, c0, c0 + 1)
            xws.append((xf * jnp.take_along_axis(w_end, idx, axis=1)).astype(BF16))
            decs.append(jnp.where(left[0:1, :], jnp.broadcast_to(dec[:, c0:c0 + 1], (1, LANES)),
                                  jnp.broadcast_to(dec[:, c0 + 1:c0 + 2], (1, LANES))))
        emit(g, jnp.concatenate(ys, axis=1))
        new = lax.dot_general(bmg, jnp.concatenate(xws, axis=1), (((0,), (0,)), ((), ())),
                              preferred_element_type=F32)
        st_ref[g] = h_in * jnp.concatenate(decs, axis=1) + new


def _ssd_fwd_kernel(xs_ref, bm_ref, cm_ref, dt_ref, alog_ref, dsk_ref, y_ref, st_ref):
    @pl.when(pl.program_id(1) == 0)
    def _():
        st_ref[...] = jnp.zeros_like(st_ref)

    def emit(g, y):
        y_ref[:, g * SSD_GROUP_WIDTH:(g + 1) * SSD_GROUP_WIDTH] = y

    _ssd_chunk(0, xs_ref, bm_ref, cm_ref, dt_ref, alog_ref, st_ref, emit, skip_ref=dsk_ref,
               issue_ahead=True)


def _ssd_bwd_kernel(final, xs_ref, bm_ref, cm_ref, dt_ref, alog_ref, yf_ref, z_ref, h_ref, mod_ref,
                    nw_ref, wo_ref, fw_ref, o_ref, st_ref, u_ref):
    @pl.when(pl.program_id(1) == 0)
    def _():
        st_ref[...] = jnp.zeros_like(st_ref)

    def emit(g, y):
        sl = slice(g * SSD_GROUP_WIDTH, (g + 1) * SSD_GROUP_WIDTH)
        u = (y + yf_ref[:, sl]) * _silu(z_ref[:, sl].astype(F32))
        u_ref[:, sl] = (_rms_normalize(u) * nw_ref[:, sl]).astype(BF16)

    _ssd_chunk(1, xs_ref, bm_ref, cm_ref, dt_ref, alog_ref, st_ref, emit)
    y = jnp.dot(u_ref[...], wo_ref[...], preferred_element_type=F32)
    h = h_ref[...] + mod_ref[2:3, :] * y
    o_ref[...] = _rms_normalize(h) * fw_ref[...] if final else h


def _ssd_scans(hs, mods, layer, z, xs, bm, cm, dt, a_log, d_skip, norm_w, w_out, final_w):
    bsz, t, d = hs.shape
    width = xs.shape[-1]
    gs_width = bm.shape[-1]
    q = SSD_CHUNK
    n_chunks = t // q
    ctx_chunks = CTX_LEN // q
    fwd = lambda b, i: (b, i, 0)
    bwd_chunk = lambda i: jnp.where(i < ctx_chunks, ctx_chunks - 1 - i, n_chunks - 1 + ctx_chunks - i)
    bwd = lambda b, i: (b, bwd_chunk(i), 0)
    const = lambda a: pl.BlockSpec(a.shape, lambda b, i: (0,) * a.ndim)
    state = pltpu.VMEM((SSD_GROUPS, SSD_STATE, SSD_GROUP_WIDTH), F32)
    chunk_specs = lambda imap: [
        pl.BlockSpec((None, q, width), imap), pl.BlockSpec((None, q, gs_width), imap),
        pl.BlockSpec((None, q, gs_width), imap), pl.BlockSpec((None, q, LANES), imap)]

    y_f = pl.pallas_call(
        _ssd_fwd_kernel,
        grid=(bsz, n_chunks),
        in_specs=[*chunk_specs(fwd), const(a_log), const(d_skip)],
        out_specs=pl.BlockSpec((None, q, width), fwd),
        out_shape=jax.ShapeDtypeStruct((bsz, t, width), F32),
        scratch_shapes=[state],
        compiler_params=pltpu.CompilerParams(
            dimension_semantics=("parallel", "arbitrary"), vmem_limit_bytes=VMEM_LIMIT),
        name="ssd_fwd",
    )(xs, bm, cm, dt, a_log, d_skip)

    final = final_w is not None
    if final:
        out_rows = t - CTX_LEN
        out_map = lambda b, i: (b, bwd_chunk(jnp.maximum(i, ctx_chunks)) - ctx_chunks, 0)
    else:
        final_w = jnp.zeros((1, d), F32)
        out_rows = t
        out_map = bwd
    return pl.pallas_call(
        functools.partial(_ssd_bwd_kernel, final),
        grid=(bsz, n_chunks),
        in_specs=[
            *chunk_specs(bwd), const(a_log),
            pl.BlockSpec((None, q, width), bwd),
            pl.BlockSpec((None, q, width), bwd),
            pl.BlockSpec((None, q, d), bwd),
            pl.BlockSpec((None, None, 3, d),
                         lambda b, i: (layer, _mod_row(bwd_chunk(i) < ctx_chunks, b), 0, 0)),
            const(norm_w), const(w_out), const(final_w),
        ],
        out_specs=pl.BlockSpec((None, q, d), out_map),
        out_shape=jax.ShapeDtypeStruct((bsz, out_rows, d), F32),
        scratch_shapes=[state, pltpu.VMEM((q, width), BF16)],
        compiler_params=pltpu.CompilerParams(
            dimension_semantics=("parallel", "arbitrary"), vmem_limit_bytes=VMEM_LIMIT),
        name="ssd_bwd",
    )(xs, bm, cm, dt, a_log, y_f, z, hs, mods, norm_w, w_out, final_w)


def _rope_tables(seq_len):
    quarter = HEAD_DIM // 4
    pos = jnp.arange(seq_len, dtype=jnp.int32)
    row = (pos // GRID_W).astype(F32)
    col = (pos % GRID_W).astype(F32)
    inv_freq = ROPE_BASE ** (-jnp.arange(0, 2 * quarter, 2, dtype=F32) / (2 * quarter))
    lane = jnp.arange(LANES)
    in_head = lane % HEAD_DIM
    p = jnp.where((in_head < HEAD_DIM // 2)[None, :], row[:, None], col[:, None])
    ang = p * inv_freq[lane % quarter][None, :]
    first = ((lane % (2 * quarter)) < quarter)[None, :]
    cos = jnp.cos(ang)
    sin = jnp.sin(ang)
    sa = jnp.where(first, -sin, 0.0)
    sb = jnp.where(first, 0.0, sin)
    ident = jnp.ones((CTX_LEN, LANES), F32)
    zeros = jnp.zeros((CTX_LEN, LANES), F32)
    return (jnp.concatenate([ident, cos], 0), jnp.concatenate([zeros, sa], 0),
            jnp.concatenate([zeros, sb], 0))


def kernel(x, c, ctx, c_ctx, w_ada, b_ada, attn_w_in, attn_sink, attn_w_out, ssd_w_in, ssd_conv_w,
           ssd_conv_b, ssd_dt_bias, ssd_a_log, ssd_d, ssd_norm_w, ssd_w_out, final_norm_w):
    bsz, seq_len, d = x.shape
    depth = w_ada.shape[0]
    assert ctx.shape[1] == CTX_LEN and CTX_LEN % ROW_TILE == 0 and seq_len % ROW_TILE == 0
    assert bsz <= 8 and depth % 2 == 0

    cond = jnp.zeros((16, d), F32).at[:bsz].set(c).at[8].set(c_ctx)
    mods = _modulation_tables(cond, w_ada, b_ada).reshape(depth, 16, 3, d)
    cos, sa, sb = _rope_tables(seq_len)
    hs = (ctx, x)

    pad_lanes = lambda a: jnp.pad(a, ((0, 0), (0, LANES - a.shape[-1])))

    for i in range(depth):
        j = i // 2
        if i % 2 == 0:
            q, k, v, g = _attn_in(hs, mods, i, attn_w_in[j].astype(BF16), cos, sa, sb)
            hs = _attn_core(hs, mods, i, q, k, v, g, attn_sink[j], attn_w_out[j].astype(BF16))
        else:
            w_in = jnp.pad(ssd_w_in[j], ((0, 0), (0, LANES - 2 * SSD_HEADS))).astype(BF16)
            z, xs, bm, cm, dt = _ssd_in(
                hs, mods, i, w_in, ssd_conv_w[j], ssd_conv_b[j][None, :],
                pad_lanes(ssd_dt_bias[j].reshape(1, -1)))
            hs = _ssd_scans(
                hs, mods, i, z, xs, bm, cm, dt, pad_lanes(ssd_a_log[j].reshape(1, -1)),
                jnp.repeat(ssd_d[j], HEAD_DIM)[None, :], ssd_norm_w[j][None, :],
                ssd_w_out[j].astype(BF16), final_norm_w[None, :] if i == depth - 1 else None)
    return hs
```

```python
import functools
import math

import jax
import jax.numpy as jnp
from jax import lax
from jax.experimental import pallas as pl
from jax.experimental.pallas import tpu as pltpu

F32 = jnp.float32
BF16 = jnp.bfloat16

EPS = 1e-6
CTX_LEN = 256
GRID_W = 64
ROPE_BASE = 10000.0

HEAD_DIM = 64
ATTN_HEADS = 16
ATTN_KV_HEADS = 4
ATTN_GROUP = ATTN_HEADS // ATTN_KV_HEADS
ATTN_BLOCK = 128
LANES = 128
SUBLANES = 8

SSD_HEADS = 32
SSD_GROUPS = 8
SSD_HEADS_PER_GROUP = SSD_HEADS // SSD_GROUPS
SSD_STATE = 128
SSD_CONV = 5
SSD_CHUNK = 128
SSD_GROUP_WIDTH = SSD_HEADS_PER_GROUP * HEAD_DIM

ROW_TILE = 256
CONV_COL_TILE = 512
MASKED = -1e30
VMEM_LIMIT = 56 * 1024 * 1024


def _silu(x):
    return x / (1.0 + jnp.exp2(x * -math.log2(math.e)))


def _rms_normalize(x):
    ms = jnp.mean(x * x, axis=-1, keepdims=True)
    return x * lax.rsqrt(ms + EPS)


def _norm_modulate(x, mod_ref):
    return _rms_normalize(x) * (1.0 + mod_ref[1:2, :]) + mod_ref[0:1, :]


def _mod_row(is_ctx, b):
    return jnp.where(is_ctx, 8, b)


def _mod_kernel(cond_ref, w_ref, b_ref, o_ref):
    cnd = cond_ref[...]
    o_ref[...] = jnp.dot(_silu(cnd), w_ref[...], preferred_element_type=F32,
                         precision=lax.Precision.HIGHEST) + b_ref[...]


def _modulation_tables(cond, w_ada, b_ada):
    depth, d, d3 = w_ada.shape
    return pl.pallas_call(
        _mod_kernel,
        grid=(depth, d3 // d),
        in_specs=[
            pl.BlockSpec((16, d), lambda l, j: (0, 0)),
            pl.BlockSpec((None, d, d), lambda l, j: (l, 0, j)),
            pl.BlockSpec((None, 1, d), lambda l, j: (l, 0, j)),
        ],
        out_specs=pl.BlockSpec((None, 16, d), lambda l, j: (l, 0, j)),
        out_shape=jax.ShapeDtypeStruct((depth, 16, d3), F32),
        compiler_params=pltpu.CompilerParams(vmem_limit_bytes=VMEM_LIMIT),
        name="adaln_tables",
    )(cond, w_ada, b_ada.reshape(depth, 1, d3))


def _attn_in_kernel(split, *refs):
    if split:
        c_ref, h_ref, *refs = refs
        h = jnp.where(pl.program_id(1) < CTX_LEN // ROW_TILE, c_ref[...], h_ref[...])
    else:
        h_ref, *refs = refs
        h = h_ref[...]
    mod_ref, w_ref, cos_ref, sa_ref, sb_ref, q_ref, k_ref, v_ref, g_ref = refs
    d = h_ref.shape[-1]
    kvw = ATTN_KV_HEADS * HEAD_DIM
    ub = _norm_modulate(h, mod_ref).astype(BF16)
    cos = cos_ref[...]
    sa = sa_ref[...]
    sb = sb_ref[...]

    def rope(t):
        return t * cos + pltpu.roll(t, LANES - 16, 1) * sa + pltpu.roll(t, 16, 1) * sb

    left = lax.broadcasted_iota(jnp.int32, (h_ref.shape[0], LANES), 1) < HEAD_DIM

    def store_doubled(ref, j, t):
        sw = pltpu.roll(t, HEAD_DIM, 1)
        ref[:, (2 * j) * LANES:(2 * j + 1) * LANES] = jnp.where(left, t, sw).astype(BF16)
        ref[:, (2 * j + 1) * LANES:(2 * j + 2) * LANES] = jnp.where(left, sw, t).astype(BF16)

    q = jnp.dot(ub, w_ref[:, 0:d], preferred_element_type=F32)
    scale = HEAD_DIM ** -0.5 * math.log2(math.e)
    for j in range(d // LANES):
        sl = slice(j * LANES, (j + 1) * LANES)
        q_ref[:, sl] = (rope(q[:, sl]) * scale).astype(BF16)
    k = jnp.dot(ub, w_ref[:, d:d + kvw], preferred_element_type=F32)
    v = jnp.dot(ub, w_ref[:, d + kvw:d + 2 * kvw], preferred_element_type=F32)
    for j in range(kvw // LANES):
        sl = slice(j * LANES, (j + 1) * LANES)
        store_doubled(k_ref, j, rope(k[:, sl]))
        store_doubled(v_ref, j, v[:, sl])
    g = jnp.dot(ub, w_ref[:, d + 2 * kvw:], preferred_element_type=F32)
    g_ref[...] = _silu(g).astype(BF16)


def _stream_specs(hs, rows):
    if not isinstance(hs, tuple):
        return [pl.BlockSpec((None, rows, hs.shape[-1]), lambda b, i: (b, i, 0))], [hs], hs.shape[1]
    ctx, x = hs
    n_ctx = CTX_LEN // rows
    d = x.shape[-1]
    return ([pl.BlockSpec((None, rows, d), lambda b, i: (b, jnp.minimum(i, n_ctx - 1), 0)),
             pl.BlockSpec((None, rows, d), lambda b, i: (b, jnp.maximum(i - n_ctx, 0), 0))],
            [ctx, x], CTX_LEN + x.shape[1])


def _attn_in(hs, mods, layer, w_in, cos, sa, sb):
    stream_specs, stream, t = _stream_specs(hs, ROW_TILE)
    bsz, _, d = stream[0].shape
    kvw = ATTN_KV_HEADS * HEAD_DIM
    n_in = w_in.shape[1]
    ctx_tiles = CTX_LEN // ROW_TILE
    row_spec = lambda width: pl.BlockSpec((None, ROW_TILE, width), lambda b, i: (b, i, 0))
    tab_spec = pl.BlockSpec((ROW_TILE, LANES), lambda b, i: (i, 0))
    return pl.pallas_call(
        functools.partial(_attn_in_kernel, len(stream) == 2),
        grid=(bsz, t // ROW_TILE),
        in_specs=[
            *stream_specs,
            pl.BlockSpec((None, None, 3, d), lambda b, i: (layer, _mod_row(i < ctx_tiles, b), 0, 0)),
            pl.BlockSpec((d, n_in), lambda b, i: (0, 0)),
            tab_spec, tab_spec, tab_spec,
        ],
        out_specs=[row_spec(d), row_spec(2 * kvw), row_spec(2 * kvw), row_spec(d)],
        out_shape=[
            jax.ShapeDtypeStruct((bsz, t, d), BF16),
            jax.ShapeDtypeStruct((bsz, t, 2 * kvw), BF16),
            jax.ShapeDtypeStruct((bsz, t, 2 * kvw), BF16),
            jax.ShapeDtypeStruct((bsz, t, d), BF16),
        ],
        compiler_params=pltpu.CompilerParams(
            dimension_semantics=("parallel", "parallel"), vmem_limit_bytes=VMEM_LIMIT),
        name="attn_in",
    )(*stream, mods, w_in, cos, sa, sb)


def _attn_core_kernel(n_lat_blocks, split, sink_ref, q_ref, kp_ref, kc_ref, kn_ref, kx_ref,
                      vp_ref, vc_ref, vn_ref, vx_ref, g_ref, *refs):
    if split:
        c_ref, h_ref, mod_ref, wo_ref, o_ref, kcat, vdiag = refs
    else:
        h_ref, mod_ref, wo_ref, o_ref, kcat, vdiag = refs
        c_ref = h_ref
    blk = pl.program_id(1)
    ctx_blocks = CTX_LEN // ATTN_BLOCK
    j = blk - ctx_blocks
    nb = ATTN_BLOCK
    pairs = ATTN_GROUP // 2
    log2e = math.log2(math.e)
    nt_dims = (((1,), (1,)), ((), ()))

    def left(rows):
        return lax.broadcasted_iota(jnp.int32, (rows, LANES), 1) < HEAD_DIM

    def split_heads(t):
        zero = jnp.zeros_like(t)
        return jnp.where(left(t.shape[0]), t, zero), jnp.where(left(t.shape[0]), zero, t)

    def stacked_queries(kh):
        rows = []
        for pr in range(pairs):
            rows.extend(split_heads(q_ref[:, (kh * pairs + pr) * LANES:(kh * pairs + pr + 1) * LANES]))
        return jnp.concatenate(rows, axis=0)

    def softmax(kh, s):
        sink = jnp.concatenate(
            [jnp.full((nb, 1), sink_ref[kh * ATTN_GROUP + r] * log2e, F32) for r in range(ATTN_GROUP)],
            axis=0)
        m = jnp.maximum(jnp.max(s, axis=-1, keepdims=True), sink)
        p = jnp.exp2(s - m)
        den = jnp.sum(p, axis=-1, keepdims=True) + jnp.exp2(sink - m)
        return p.astype(BF16), 1.0 / den

    def weighted_values(kh, pb, r, n_keys):
        lhs = jnp.concatenate(
            [jnp.concatenate([pb[2 * pr * nb:(2 * pr + 1) * nb], pb[(2 * pr + 1) * nb:(2 * pr + 2) * nb]],
                             axis=1) for pr in range(pairs)], axis=0)
        o = jnp.dot(lhs, vdiag[kh, 0:2 * n_keys, :], preferred_element_type=F32)
        gated = []
        for pr in range(pairs):
            lo, mid, hi = 2 * pr * nb, (2 * pr + 1) * nb, (2 * pr + 2) * nb
            rr = jnp.where(left(nb), r[lo:mid], r[mid:hi])
            sl = slice((kh * pairs + pr) * LANES, (kh * pairs + pr + 1) * LANES)
            gated.append((o[pr * nb:(pr + 1) * nb] * rr * g_ref[:, sl].astype(F32)).astype(BF16))
        rows = slice(kh * pairs * LANES, (kh + 1) * pairs * LANES)
        return jnp.dot(jnp.concatenate(gated, axis=1), wo_ref[rows, :], preferred_element_type=F32)

    def attend(key_blocks, bias, res_ref):
        n_keys = sum(k_ref.shape[0] for k_ref, _ in key_blocks)
        r0 = 0
        for k_ref, v_ref in key_blocks:
            r1 = r0 + k_ref.shape[0]
            kcat[r0:r1, :] = k_ref[...]
            for kh in range(ATTN_KV_HEADS):
                top, bot = split_heads(v_ref[:, kh * LANES:(kh + 1) * LANES])
                vdiag[kh, r0:r1, :] = top
                vdiag[kh, n_keys + r0:n_keys + r1, :] = bot
            r0 = r1

        def logits(kh):
            s = lax.dot_general(stacked_queries(kh), kcat[0:n_keys, kh * LANES:(kh + 1) * LANES],
                                nt_dims, preferred_element_type=F32)
            if not bias:
                return s
            return jnp.concatenate(
                [s[:, i * nb:(i + 1) * nb] + bias[i] if i in bias else s[:, i * nb:(i + 1) * nb]
                 for i in range(n_keys // nb)], axis=1)

        s_next = logits(0)
        pending = None
        y = None
        for kh in range(ATTN_KV_HEADS):
            s = s_next
            if kh + 1 < ATTN_KV_HEADS:
                s_next = logits(kh + 1)
            pb, r = softmax(kh, s)
            if pending is not None:
                part = weighted_values(*pending, n_keys)
                y = part if y is None else y + part
            pending = (kh, pb, r)
        y = y + weighted_values(*pending, n_keys)
        o_ref[...] = res_ref[...] + mod_ref[2:3, :] * y

    @pl.when(blk >= ctx_blocks)
    def _():
        row = lax.broadcasted_iota(jnp.int32, (nb, nb), 0)
        col = lax.broadcasted_iota(jnp.int32, (nb, nb), 1)
        bias_prev = jnp.where((col >= row) & (j > 0), 0.0, MASKED).astype(F32)
        bias_next = jnp.where((col <= row) & (j < n_lat_blocks - 1), 0.0, MASKED).astype(F32)
        attend(((kp_ref, vp_ref), (kc_ref, vc_ref), (kn_ref, vn_ref), (kx_ref, vx_ref)),
               {0: jnp.concatenate([bias_prev] * ATTN_GROUP, axis=0),
                2: jnp.concatenate([bias_next] * ATTN_GROUP, axis=0)}, h_ref)

    @pl.when(blk < ctx_blocks)
    def _():
        attend(((kx_ref, vx_ref),), {}, c_ref)


def _attn_core(hs, mods, layer, q, k, v, g, sink, w_out):
    nb = ATTN_BLOCK
    stream_specs, stream, t = _stream_specs(hs, nb)
    bsz, _, d = stream[0].shape
    kvw = ATTN_KV_HEADS * LANES
    ctx_blocks = CTX_LEN // nb
    n_blocks = t // nb
    n_lat = n_blocks - ctx_blocks
    blk_spec = lambda width, imap: pl.BlockSpec((None, nb, width), imap)
    own = lambda b, i: (b, i, 0)
    prev = lambda b, i: (b, jnp.maximum(i - 1, ctx_blocks), 0)
    nxt = lambda b, i: (b, jnp.minimum(jnp.maximum(i + 1, ctx_blocks), n_blocks - 1), 0)
    ctx_spec = pl.BlockSpec((None, CTX_LEN, kvw), lambda b, i: (b, 0, 0))
    kv_specs = [blk_spec(kvw, prev), blk_spec(kvw, own), blk_spec(kvw, nxt), ctx_spec]
    return pl.pallas_call(
        functools.partial(_attn_core_kernel, n_lat, len(stream) == 2),
        grid=(bsz, n_blocks),
        in_specs=[
            pl.BlockSpec(memory_space=pltpu.SMEM),
            blk_spec(d, own),
            *kv_specs, *kv_specs,
            blk_spec(d, own),
            *stream_specs,
            pl.BlockSpec((None, None, 3, d), lambda b, i: (layer, _mod_row(i < ctx_blocks, b), 0, 0)),
            pl.BlockSpec(w_out.shape, lambda b, i: (0, 0)),
        ],
        out_specs=blk_spec(d, own),
        out_shape=jax.ShapeDtypeStruct((bsz, t, d), F32),
        scratch_shapes=[
            pltpu.VMEM((3 * nb + CTX_LEN, kvw), BF16),
            pltpu.VMEM((ATTN_KV_HEADS, 2 * (3 * nb + CTX_LEN), LANES), BF16),
        ],
        compiler_params=pltpu.CompilerParams(
            dimension_semantics=("parallel", "parallel"), vmem_limit_bytes=VMEM_LIMIT),
        name="attn_core",
    )(sink, q, k, k, k, k, v, v, v, v, g, *stream, mods, w_out)


def _ssd_in_kernel(h_ref, hp_ref, hn_ref, mod_ref, w_ref, cw_ref, cb_ref, dtb_ref,
                   z_ref, xs_ref, bm_ref, cm_ref, dt_ref):
    blk = pl.program_id(1)
    n_tiles = pl.num_programs(1)
    ctx_tiles = CTX_LEN // ROW_TILE
    halo = SUBLANES
    tm = ROW_TILE
    width = xs_ref.shape[-1]
    gs_width = bm_ref.shape[-1]

    prev_ok = (blk > ctx_tiles) | ((blk > 0) & (blk < ctx_tiles))
    next_ok = ((blk >= ctx_tiles) & (blk < n_tiles - 1)) | (blk < ctx_tiles - 1)
    u_main = _norm_modulate(h_ref[...], mod_ref)
    u_all = jnp.concatenate(
        [_norm_modulate(hp_ref[...], mod_ref) * prev_ok.astype(F32), u_main,
         _norm_modulate(hn_ref[...], mod_ref) * next_ok.astype(F32)], axis=0).astype(BF16)
    ub = u_main.astype(BF16)

    conv_dim = cw_ref.shape[-1]
    z_ref[...] = jnp.dot(ub, w_ref[:, 0:width], preferred_element_type=F32).astype(BF16)
    dt_raw = jnp.dot(ub, w_ref[:, width + conv_dim:], preferred_element_type=F32) + dtb_ref[...]
    dt_ref[...] = jnp.maximum(dt_raw, 0.0) + jnp.log1p(jnp.exp(-jnp.abs(dt_raw)))

    pad = SSD_CONV // 2
    n_rows = tm + 2 * halo
    sub = lax.broadcasted_iota(jnp.int32, (1, SUBLANES, CONV_COL_TILE), 1)
    for c in range(conv_dim // CONV_COL_TILE):
        cs = slice(c * CONV_COL_TILE, (c + 1) * CONV_COL_TILE)
        pre = jnp.dot(u_all, w_ref[:, width + c * CONV_COL_TILE:width + (c + 1) * CONV_COL_TILE],
                      preferred_element_type=F32)
        tiles = pre.reshape(n_rows // SUBLANES, SUBLANES, CONV_COL_TILE)
        n_out = tm // SUBLANES
        own = tiles[1:1 + n_out]
        acc = cb_ref[:, cs] + cw_ref[pad:pad + 1, cs] * own
        for kk in range(SSD_CONV):
            off = kk - pad
            if off == 0:
                continue
            from_own = (sub >= off) if off > 0 else (sub < SUBLANES + off)
            nbr = 1 + (1 if off > 0 else -1)
            mixed = jnp.where(from_own, own, tiles[nbr:nbr + n_out])
            acc = acc + cw_ref[kk:kk + 1, cs] * pltpu.roll(mixed, (-off) % SUBLANES, 1)
        out = _silu(acc).reshape(tm, CONV_COL_TILE).astype(BF16)
        lo = c * CONV_COL_TILE
        if lo < width:
            xs_ref[:, lo:lo + CONV_COL_TILE] = out
        elif lo < width + gs_width:
            bm_ref[:, lo - width:lo - width + CONV_COL_TILE] = out
        else:
            cm_ref[:, lo - width - gs_width:lo - width - gs_width + CONV_COL_TILE] = out


def _ssd_in(hs, mods, layer, w_in, conv_w, conv_b, dt_bias):
    bsz, t, d = hs.shape
    width = w_in.shape[1] - conv_w.shape[1] - LANES
    gs_width = SSD_GROUPS * SSD_STATE
    n_tiles = t // ROW_TILE
    ctx_tiles = CTX_LEN // ROW_TILE
    per_tile = ROW_TILE // SUBLANES
    n_halo_blocks = t // SUBLANES
    row_spec = lambda w_: pl.BlockSpec((None, ROW_TILE, w_), lambda b, i: (b, i, 0))
    full = lambda a: pl.BlockSpec(a.shape, lambda b, i: (0,) * a.ndim)
    return pl.pallas_call(
        _ssd_in_kernel,
        grid=(bsz, n_tiles),
        in_specs=[
            row_spec(d),
            pl.BlockSpec((None, SUBLANES, d), lambda b, i: (b, jnp.maximum(i * per_tile - 1, 0), 0)),
            pl.BlockSpec((None, SUBLANES, d),
                         lambda b, i: (b, jnp.minimum((i + 1) * per_tile, n_halo_blocks - 1), 0)),
            pl.BlockSpec((None, None, 3, d), lambda b, i: (layer, _mod_row(i < ctx_tiles, b), 0, 0)),
            full(w_in), full(conv_w), full(conv_b), full(dt_bias),
        ],
        out_specs=[row_spec(width), row_spec(width), row_spec(gs_width), row_spec(gs_width),
                   row_spec(LANES)],
        out_shape=[
            jax.ShapeDtypeStruct((bsz, t, width), BF16),
            jax.ShapeDtypeStruct((bsz, t, width), BF16),
            jax.ShapeDtypeStruct((bsz, t, gs_width), BF16),
            jax.ShapeDtypeStruct((bsz, t, gs_width), BF16),
            jax.ShapeDtypeStruct((bsz, t, LANES), F32),
        ],
        compiler_params=pltpu.CompilerParams(
            dimension_semantics=("parallel", "parallel"), vmem_limit_bytes=VMEM_LIMIT),
        name="ssd_in",
    )(hs, hs, hs, mods, w_in, conv_w, conv_b, dt_bias)


def _ssd_chunk(direction, xs_ref, bm_ref, cm_ref, dt_ref, alog_ref, st_ref, emit, skip_ref=None,
               issue_ahead=False):
    q = SSD_CHUNK
    log2e = math.log2(math.e)
    dt = dt_ref[...]
    da = dt * (-jnp.exp(alog_ref[...]))
    row = lax.broadcasted_iota(jnp.int32, (q, q), 0)
    col = lax.broadcasted_iota(jnp.int32, (q, q), 1)
    seen = (row >= col) if direction == 0 else (col >= row)
    a = jnp.dot(seen.astype(F32), da, preferred_element_type=F32,
                precision=lax.Precision.HIGHEST)
    last = q - 1 if direction == 0 else 0
    a_tot = a[last:last + 1, :]
    w_end = jnp.exp(a_tot - a) * dt
    dec = jnp.exp(a_tot)
    a2 = a * log2e
    src2_t = ((a - jnp.log(dt)) * log2e).T
    left = col < HEAD_DIM

    def group_products(g):
        gsl = slice(g * SSD_STATE, (g + 1) * SSD_STATE)
        cmg = cm_ref[:, gsl]
        bmg = bm_ref[:, gsl]
        cb = lax.dot_general(cmg, bmg, (((1,), (1,)), ((), ())), preferred_element_type=F32)
        h_in = st_ref[g]
        y_off = jnp.dot(cmg, h_in.astype(BF16), preferred_element_type=F32)
        return bmg, cb, h_in, y_off

    ahead = group_products(0) if issue_ahead else None
    for g in range(SSD_GROUPS):
        bmg, cb, h_in, y_off = ahead if issue_ahead else group_products(g)
        if issue_ahead and g + 1 < SSD_GROUPS:
            ahead = group_products(g + 1)
        ys, xws, decs = [], [], []
        for pr in range(SSD_HEADS_PER_GROUP // 2):
            pair = g * (SSD_HEADS_PER_GROUP // 2) + pr
            c0 = direction * SSD_HEADS + 2 * pair
            psl = slice(pair * LANES, (pair + 1) * LANES)
            xp = xs_ref[:, psl]
            ws, eas = [], []
            for c in (c0, c0 + 1):
                a_col = jnp.broadcast_to(a2[:, c:c + 1], (q, q))
                lmat = jnp.exp2(jnp.where(seen, a_col - src2_t[c:c + 1, :], -jnp.inf))
                ws.append((cb * lmat).astype(BF16))
                eas.append(jnp.exp2(a_col))
            zero = jnp.zeros_like(xp)
            x_diag = jnp.concatenate([jnp.where(left, xp, zero), jnp.where(left, zero, xp)], axis=0)
            y_diag = jnp.dot(jnp.concatenate(ws, axis=1), x_diag, preferred_element_type=F32)
            y = y_diag + y_off[:, pr * LANES:(pr + 1) * LANES] * jnp.where(left, eas[0], eas[1])
            xf = xp.astype(F32)
            if skip_ref is not None:
                y = y + skip_ref[:, psl] * xf
            ys.append(y)
            idx = jnp.where(left, c0, c0 + 1)
            xws.append((xf * jnp.take_along_axis(w_end, idx, axis=1)).astype(BF16))
            decs.append(jnp.where(left[0:1, :], jnp.broadcast_to(dec[:, c0:c0 + 1], (1, LANES)),
                                  jnp.broadcast_to(dec[:, c0 + 1:c0 + 2], (1, LANES))))
        emit(g, jnp.concatenate(ys, axis=1))
        new = lax.dot_general(bmg, jnp.concatenate(xws, axis=1), (((0,), (0,)), ((), ())),
                              preferred_element_type=F32)
        st_ref[g] = h_in * jnp.concatenate(decs, axis=1) + new


def _ssd_fwd_kernel(xs_ref, bm_ref, cm_ref, dt_ref, alog_ref, dsk_ref, y_ref, st_ref):
    @pl.when(pl.program_id(1) == 0)
    def _():
        st_ref[...] = jnp.zeros_like(st_ref)

    def emit(g, y):
        y_ref[:, g * SSD_GROUP_WIDTH:(g + 1) * SSD_GROUP_WIDTH] = y

    _ssd_chunk(0, xs_ref, bm_ref, cm_ref, dt_ref, alog_ref, st_ref, emit, skip_ref=dsk_ref,
               issue_ahead=True)


def _ssd_bwd_kernel(final, xs_ref, bm_ref, cm_ref, dt_ref, alog_ref, yf_ref, z_ref, h_ref, mod_ref,
                    nw_ref, wo_ref, fw_ref, o_ref, st_ref, u_ref):
    step = pl.program_id(1)

    @pl.when(step == 0)
    def _():
        st_ref[...] = jnp.zeros_like(st_ref)
        u_ref[...] = jnp.zeros_like(u_ref)

    y = jnp.dot(u_ref[(step + 1) % 2], wo_ref[...], preferred_element_type=F32)
    h = h_ref[...] + mod_ref[2:3, :] * y
    o_ref[...] = _rms_normalize(h) * fw_ref[...] if final else h

    def emit(g, y):
        sl = slice(g * SSD_GROUP_WIDTH, (g + 1) * SSD_GROUP_WIDTH)
        u = (y + yf_ref[:, sl]) * _silu(z_ref[:, sl].astype(F32))
        u_ref[step % 2, :, sl] = (_rms_normalize(u) * nw_ref[:, sl]).astype(BF16)

    _ssd_chunk(1, xs_ref, bm_ref, cm_ref, dt_ref, alog_ref, st_ref, emit)


def _ssd_scans(hs, mods, layer, z, xs, bm, cm, dt, a_log, d_skip, norm_w, w_out, final_w):
    bsz, t, d = hs.shape
    width = xs.shape[-1]
    gs_width = bm.shape[-1]
    q = SSD_CHUNK
    n_chunks = t // q
    ctx_chunks = CTX_LEN // q
    fwd = lambda b, i: (b, i, 0)
    bwd_chunk = lambda i: jnp.where(i < ctx_chunks, ctx_chunks - 1 - i, n_chunks - 1 + ctx_chunks - i)
    const = lambda a: pl.BlockSpec(a.shape, lambda b, i: (0,) * a.ndim)
    state = pltpu.VMEM((SSD_GROUPS, SSD_STATE, SSD_GROUP_WIDTH), F32)
    chunk_specs = lambda imap: [
        pl.BlockSpec((None, q, width), imap), pl.BlockSpec((None, q, gs_width), imap),
        pl.BlockSpec((None, q, gs_width), imap), pl.BlockSpec((None, q, LANES), imap)]

    y_f = pl.pallas_call(
        _ssd_fwd_kernel,
        grid=(bsz, n_chunks),
        in_specs=[*chunk_specs(fwd), const(a_log), const(d_skip)],
        out_specs=pl.BlockSpec((None, q, width), fwd),
        out_shape=jax.ShapeDtypeStruct((bsz, t, width), F32),
        scratch_shapes=[state],
        compiler_params=pltpu.CompilerParams(
            dimension_semantics=("parallel", "arbitrary"), vmem_limit_bytes=VMEM_LIMIT),
        name="ssd_fwd",
    )(xs, bm, cm, dt, a_log, d_skip)

    scanned = lambda b, i: (b, bwd_chunk(jnp.minimum(i, n_chunks - 1)), 0)
    behind = lambda i: bwd_chunk(jnp.maximum(i - 1, 0))
    final = final_w is not None
    if final:
        out_rows = t - CTX_LEN
        out_map = lambda b, i: (b, bwd_chunk(jnp.maximum(i - 1, ctx_chunks)) - ctx_chunks, 0)
    else:
        final_w = jnp.zeros((1, d), F32)
        out_rows = t
        out_map = lambda b, i: (b, behind(i), 0)
    return pl.pallas_call(
        functools.partial(_ssd_bwd_kernel, final),
        grid=(bsz, n_chunks + 1),
        in_specs=[
            *chunk_specs(scanned), const(a_log),
            pl.BlockSpec((None, q, width), scanned),
            pl.BlockSpec((None, q, width), scanned),
            pl.BlockSpec((None, q, d), lambda b, i: (b, behind(i), 0)),
            pl.BlockSpec((None, None, 3, d),
                         lambda b, i: (layer, _mod_row(behind(i) < ctx_chunks, b), 0, 0)),
            const(norm_w), const(w_out), const(final_w),
        ],
        out_specs=pl.BlockSpec((None, q, d), out_map),
        out_shape=jax.ShapeDtypeStruct((bsz, out_rows, d), F32),
        scratch_shapes=[state, pltpu.VMEM((2, q, width), BF16)],
        compiler_params=pltpu.CompilerParams(
            dimension_semantics=("parallel", "arbitrary"), vmem_limit_bytes=VMEM_LIMIT),
        name="ssd_bwd",
    )(xs, bm, cm, dt, a_log, y_f, z, hs, mods, norm_w, w_out, final_w)


def _rope_tables(seq_len):
    quarter = HEAD_DIM // 4
    pos = jnp.arange(seq_len, dtype=jnp.int32)
    row = (pos // GRID_W).astype(F32)
    col = (pos % GRID_W).astype(F32)
    inv_freq = ROPE_BASE ** (-jnp.arange(0, 2 * quarter, 2, dtype=F32) / (2 * quarter))
    lane = jnp.arange(LANES)
    in_head = lane % HEAD_DIM
    p = jnp.where((in_head < HEAD_DIM // 2)[None, :], row[:, None], col[:, None])
    ang = p * inv_freq[lane % quarter][None, :]
    first = ((lane % (2 * quarter)) < quarter)[None, :]
    cos = jnp.cos(ang)
    sin = jnp.sin(ang)
    sa = jnp.where(first, -sin, 0.0)
    sb = jnp.where(first, 0.0, sin)
    ident = jnp.ones((CTX_LEN, LANES), F32)
    zeros = jnp.zeros((CTX_LEN, LANES), F32)
    return (jnp.concatenate([ident, cos], 0), jnp.concatenate([zeros, sa], 0),
            jnp.concatenate([zeros, sb], 0))


def kernel(x, c, ctx, c_ctx, w_ada, b_ada, attn_w_in, attn_sink, attn_w_out, ssd_w_in, ssd_conv_w,
           ssd_conv_b, ssd_dt_bias, ssd_a_log, ssd_d, ssd_norm_w, ssd_w_out, final_norm_w):
    bsz, seq_len, d = x.shape
    depth = w_ada.shape[0]
    assert ctx.shape[1] == CTX_LEN and CTX_LEN % ROW_TILE == 0 and seq_len % ROW_TILE == 0
    assert bsz <= 8 and depth % 2 == 0

    cond = jnp.zeros((16, d), F32).at[:bsz].set(c).at[8].set(c_ctx)
    mods = _modulation_tables(cond, w_ada, b_ada).reshape(depth, 16, 3, d)
    cos, sa, sb = _rope_tables(seq_len)
    hs = (ctx, x)

    pad_lanes = lambda a: jnp.pad(a, ((0, 0), (0, LANES - a.shape[-1])))

    for i in range(depth):
        j = i // 2
        if i % 2 == 0:
            q, k, v, g = _attn_in(hs, mods, i, attn_w_in[j].astype(BF16), cos, sa, sb)
            hs = _attn_core(hs, mods, i, q, k, v, g, attn_sink[j], attn_w_out[j].astype(BF16))
        else:
            w_in = jnp.pad(ssd_w_in[j], ((0, 0), (0, LANES - 2 * SSD_HEADS))).astype(BF16)
            z, xs, bm, cm, dt = _ssd_in(
                hs, mods, i, w_in, ssd_conv_w[j], ssd_conv_b[j][None, :],
                pad_lanes(ssd_dt_bias[j].reshape(1, -1)))
            hs = _ssd_scans(
                hs, mods, i, z, xs, bm, cm, dt, pad_lanes(ssd_a_log[j].reshape(1, -1)),
                jnp.repeat(ssd_d[j], HEAD_DIM)[None, :], ssd_norm_w[j][None, :],
                ssd_w_out[j].astype(BF16), final_norm_w[None, :] if i == depth - 1 else None)
    return hs
```

```python
import functools
import math

import jax
import jax.numpy as jnp
from jax import lax
from jax.experimental import pallas as pl
from jax.experimental.pallas import tpu as pltpu

F32 = jnp.float32
BF16 = jnp.bfloat16

EPS = 1e-6
CTX_LEN = 256
GRID_W = 64
ROPE_BASE = 10000.0

HEAD_DIM = 64
ATTN_HEADS = 16
ATTN_KV_HEADS = 4
ATTN_GROUP = ATTN_HEADS // ATTN_KV_HEADS
ATTN_BLOCK = 128
LANES = 128
SUBLANES = 8

SSD_HEADS = 32
SSD_GROUPS = 8
SSD_HEADS_PER_GROUP = SSD_HEADS // SSD_GROUPS
SSD_STATE = 128
SSD_CONV = 5
SSD_CHUNK = 128
SSD_GROUP_WIDTH = SSD_HEADS_PER_GROUP * HEAD_DIM

ROW_TILE = 256
CONV_COL_TILE = 512
MASKED = -1e30
VMEM_LIMIT = 56 * 1024 * 1024


def _silu(x):
    return x / (1.0 + jnp.exp2(x * -math.log2(math.e)))


def _rms_normalize(x):
    ms = jnp.mean(x * x, axis=-1, keepdims=True)
    return x * lax.rsqrt(ms + EPS)


def _norm_modulate(x, mod_ref):
    return _rms_normalize(x) * (1.0 + mod_ref[1:2, :]) + mod_ref[0:1, :]


def _mod_row(is_ctx, b):
    return jnp.where(is_ctx, 8, b)


def _mod_kernel(cond_ref, w_ref, b_ref, o_ref):
    cnd = cond_ref[...]
    o_ref[...] = jnp.dot(_silu(cnd), w_ref[...], preferred_element_type=F32,
                         precision=lax.Precision.HIGHEST) + b_ref[...]


def _modulation_tables(cond, w_ada, b_ada):
    depth, d, d3 = w_ada.shape
    return pl.pallas_call(
        _mod_kernel,
        grid=(depth, d3 // d),
        in_specs=[
            pl.BlockSpec((16, d), lambda l, j: (0, 0)),
            pl.BlockSpec((None, d, d), lambda l, j: (l, 0, j)),
            pl.BlockSpec((None, 1, d), lambda l, j: (l, 0, j)),
        ],
        out_specs=pl.BlockSpec((None, 16, d), lambda l, j: (l, 0, j)),
        out_shape=jax.ShapeDtypeStruct((depth, 16, d3), F32),
        compiler_params=pltpu.CompilerParams(vmem_limit_bytes=VMEM_LIMIT),
        name="adaln_tables",
    )(cond, w_ada, b_ada.reshape(depth, 1, d3))


def _attn_in_kernel(split, *refs):
    if split:
        c_ref, h_ref, *refs = refs
        h = jnp.where(pl.program_id(1) < CTX_LEN // ROW_TILE, c_ref[...], h_ref[...])
    else:
        h_ref, *refs = refs
        h = h_ref[...]
    mod_ref, w_ref, cos_ref, sa_ref, sb_ref, q_ref, k_ref, v_ref, g_ref = refs
    d = h_ref.shape[-1]
    kvw = ATTN_KV_HEADS * HEAD_DIM
    ub = _norm_modulate(h, mod_ref).astype(BF16)
    cos = cos_ref[...]
    sa = sa_ref[...]
    sb = sb_ref[...]

    def rope(t):
        return t * cos + pltpu.roll(t, LANES - 16, 1) * sa + pltpu.roll(t, 16, 1) * sb

    left = lax.broadcasted_iota(jnp.int32, (h_ref.shape[0], LANES), 1) < HEAD_DIM

    def store_doubled(ref, j, t):
        sw = pltpu.roll(t, HEAD_DIM, 1)
        ref[:, (2 * j) * LANES:(2 * j + 1) * LANES] = jnp.where(left, t, sw).astype(BF16)
        ref[:, (2 * j + 1) * LANES:(2 * j + 2) * LANES] = jnp.where(left, sw, t).astype(BF16)

    q = jnp.dot(ub, w_ref[:, 0:d], preferred_element_type=F32)
    scale = HEAD_DIM ** -0.5 * math.log2(math.e)
    for j in range(d // LANES):
        sl = slice(j * LANES, (j + 1) * LANES)
        q_ref[:, sl] = (rope(q[:, sl]) * scale).astype(BF16)
    k = jnp.dot(ub, w_ref[:, d:d + kvw], preferred_element_type=F32)
    v = jnp.dot(ub, w_ref[:, d + kvw:d + 2 * kvw], preferred_element_type=F32)
    for j in range(kvw // LANES):
        sl = slice(j * LANES, (j + 1) * LANES)
        store_doubled(k_ref, j, rope(k[:, sl]))
        store_doubled(v_ref, j, v[:, sl])
    g = jnp.dot(ub, w_ref[:, d + 2 * kvw:], preferred_element_type=F32)
    g_ref[...] = _silu(g).astype(BF16)


def _stream_specs(hs, rows):
    if not isinstance(hs, tuple):
        return [pl.BlockSpec((None, rows, hs.shape[-1]), lambda b, i: (b, i, 0))], [hs], hs.shape[1]
    ctx, x = hs
    n_ctx = CTX_LEN // rows
    d = x.shape[-1]
    return ([pl.BlockSpec((None, rows, d), lambda b, i: (b, jnp.minimum(i, n_ctx - 1), 0)),
             pl.BlockSpec((None, rows, d), lambda b, i: (b, jnp.maximum(i - n_ctx, 0), 0))],
            [ctx, x], CTX_LEN + x.shape[1])


def _attn_in(hs, mods, layer, w_in, cos, sa, sb):
    stream_specs, stream, t = _stream_specs(hs, ROW_TILE)
    bsz, _, d = stream[0].shape
    kvw = ATTN_KV_HEADS * HEAD_DIM
    n_in = w_in.shape[1]
    ctx_tiles = CTX_LEN // ROW_TILE
    row_spec = lambda width: pl.BlockSpec((None, ROW_TILE, width), lambda b, i: (b, i, 0))
    tab_spec = pl.BlockSpec((ROW_TILE, LANES), lambda b, i: (i, 0))
    return pl.pallas_call(
        functools.partial(_attn_in_kernel, len(stream) == 2),
        grid=(bsz, t // ROW_TILE),
        in_specs=[
            *stream_specs,
            pl.BlockSpec((None, None, 3, d), lambda b, i: (layer, _mod_row(i < ctx_tiles, b), 0, 0)),
            pl.BlockSpec((d, n_in), lambda b, i: (0, 0)),
            tab_spec, tab_spec, tab_spec,
        ],
        out_specs=[row_spec(d), row_spec(2 * kvw), row_spec(2 * kvw), row_spec(d)],
        out_shape=[
            jax.ShapeDtypeStruct((bsz, t, d), BF16),
            jax.ShapeDtypeStruct((bsz, t, 2 * kvw), BF16),
            jax.ShapeDtypeStruct((bsz, t, 2 * kvw), BF16),
            jax.ShapeDtypeStruct((bsz, t, d), BF16),
        ],
        compiler_params=pltpu.CompilerParams(
            dimension_semantics=("parallel", "parallel"), vmem_limit_bytes=VMEM_LIMIT),
        name="attn_in",
    )(*stream, mods, w_in, cos, sa, sb)


def _attn_core_kernel(n_lat_blocks, split, sink_ref, q_ref, kp_ref, kc_ref, kn_ref, kx_ref,
                      vp_ref, vc_ref, vn_ref, vx_ref, g_ref, *refs):
    if split:
        c_ref, h_ref, mod_ref, wo_ref, o_ref, kcat, vdiag = refs
    else:
        h_ref, mod_ref, wo_ref, o_ref, kcat, vdiag = refs
        c_ref = h_ref
    blk = pl.program_id(1)
    ctx_blocks = CTX_LEN // ATTN_BLOCK
    j = blk - ctx_blocks
    nb = ATTN_BLOCK
    pairs = ATTN_GROUP // 2
    log2e = math.log2(math.e)
    nt_dims = (((1,), (1,)), ((), ()))

    def left(rows):
        return lax.broadcasted_iota(jnp.int32, (rows, LANES), 1) < HEAD_DIM

    def split_heads(t):
        zero = jnp.zeros_like(t)
        return jnp.where(left(t.shape[0]), t, zero), jnp.where(left(t.shape[0]), zero, t)

    def stacked_queries(kh):
        rows = []
        for pr in range(pairs):
            rows.extend(split_heads(q_ref[:, (kh * pairs + pr) * LANES:(kh * pairs + pr + 1) * LANES]))
        return jnp.concatenate(rows, axis=0)

    def softmax(kh, s):
        sink = jnp.concatenate(
            [jnp.full((nb, 1), sink_ref[kh * ATTN_GROUP + r] * log2e, F32) for r in range(ATTN_GROUP)],
            axis=0)
        m = jnp.maximum(jnp.max(s, axis=-1, keepdims=True), sink)
        p = jnp.exp2(s - m)
        den = jnp.sum(p, axis=-1, keepdims=True) + jnp.exp2(sink - m)
        return p.astype(BF16), 1.0 / den

    def weighted_values(kh, pb, r, n_keys):
        lhs = jnp.concatenate(
            [jnp.concatenate([pb[2 * pr * nb:(2 * pr + 1) * nb], pb[(2 * pr + 1) * nb:(2 * pr + 2) * nb]],
                             axis=1) for pr in range(pairs)], axis=0)
        o = jnp.dot(lhs, vdiag[kh, 0:2 * n_keys, :], preferred_element_type=F32)
        gated = []
        for pr in range(pairs):
            lo, mid, hi = 2 * pr * nb, (2 * pr + 1) * nb, (2 * pr + 2) * nb
            rr = jnp.where(left(nb), r[lo:mid], r[mid:hi])
            sl = slice((kh * pairs + pr) * LANES, (kh * pairs + pr + 1) * LANES)
            gated.append((o[pr * nb:(pr + 1) * nb] * rr * g_ref[:, sl].astype(F32)).astype(BF16))
        rows = slice(kh * pairs * LANES, (kh + 1) * pairs * LANES)
        return jnp.dot(jnp.concatenate(gated, axis=1), wo_ref[rows, :], preferred_element_type=F32)

    def attend(key_blocks, bias, res_ref):
        n_keys = sum(k_ref.shape[0] for k_ref, _ in key_blocks)
        r0 = 0
        for k_ref, v_ref in key_blocks:
            r1 = r0 + k_ref.shape[0]
            kcat[r0:r1, :] = k_ref[...]
            for kh in range(ATTN_KV_HEADS):
                top, bot = split_heads(v_ref[:, kh * LANES:(kh + 1) * LANES])
                vdiag[kh, r0:r1, :] = top
                vdiag[kh, n_keys + r0:n_keys + r1, :] = bot
            r0 = r1

        def logits(kh):
            s = lax.dot_general(stacked_queries(kh), kcat[0:n_keys, kh * LANES:(kh + 1) * LANES],
                                nt_dims, preferred_element_type=F32)
            if not bias:
                return s
            return jnp.concatenate(
                [s[:, i * nb:(i + 1) * nb] + bias[i] if i in bias else s[:, i * nb:(i + 1) * nb]
                 for i in range(n_keys // nb)], axis=1)

        s_next = logits(0)
        pending = None
        y = None
        for kh in range(ATTN_KV_HEADS):
            s = s_next
            if kh + 1 < ATTN_KV_HEADS:
                s_next = logits(kh + 1)
            pb, r = softmax(kh, s)
            if pending is not None:
                part = weighted_values(*pending, n_keys)
                y = part if y is None else y + part
            pending = (kh, pb, r)
        y = y + weighted_values(*pending, n_keys)
        o_ref[...] = res_ref[...] + mod_ref[2:3, :] * y

    @pl.when(blk >= ctx_blocks)
    def _():
        row = lax.broadcasted_iota(jnp.int32, (nb, nb), 0)
        col = lax.broadcasted_iota(jnp.int32, (nb, nb), 1)
        bias_prev = jnp.where((col >= row) & (j > 0), 0.0, MASKED).astype(F32)
        bias_next = jnp.where((col <= row) & (j < n_lat_blocks - 1), 0.0, MASKED).astype(F32)
        attend(((kp_ref, vp_ref), (kc_ref, vc_ref), (kn_ref, vn_ref), (kx_ref, vx_ref)),
               {0: jnp.concatenate([bias_prev] * ATTN_GROUP, axis=0),
                2: jnp.concatenate([bias_next] * ATTN_GROUP, axis=0)}, h_ref)

    @pl.when(blk < ctx_blocks)
    def _():
        attend(((kx_ref, vx_ref),), {}, c_ref)


def _attn_core(hs, mods, layer, q, k, v, g, sink, w_out):
    nb = ATTN_BLOCK
    stream_specs, stream, t = _stream_specs(hs, nb)
    bsz, _, d = stream[0].shape
    kvw = ATTN_KV_HEADS * LANES
    ctx_blocks = CTX_LEN // nb
    n_blocks = t // nb
    n_lat = n_blocks - ctx_blocks
    blk_spec = lambda width, imap: pl.BlockSpec((None, nb, width), imap)
    own = lambda b, i: (b, i, 0)
    prev = lambda b, i: (b, jnp.maximum(i - 1, ctx_blocks), 0)
    nxt = lambda b, i: (b, jnp.minimum(jnp.maximum(i + 1, ctx_blocks), n_blocks - 1), 0)
    ctx_spec = pl.BlockSpec((None, CTX_LEN, kvw), lambda b, i: (b, 0, 0))
    kv_specs = [blk_spec(kvw, prev), blk_spec(kvw, own), blk_spec(kvw, nxt), ctx_spec]
    return pl.pallas_call(
        functools.partial(_attn_core_kernel, n_lat, len(stream) == 2),
        grid=(bsz, n_blocks),
        in_specs=[
            pl.BlockSpec(memory_space=pltpu.SMEM),
            blk_spec(d, own),
            *kv_specs, *kv_specs,
            blk_spec(d, own),
            *stream_specs,
            pl.BlockSpec((None, None, 3, d), lambda b, i: (layer, _mod_row(i < ctx_blocks, b), 0, 0)),
            pl.BlockSpec(w_out.shape, lambda b, i: (0, 0)),
        ],
        out_specs=blk_spec(d, own),
        out_shape=jax.ShapeDtypeStruct((bsz, t, d), F32),
        scratch_shapes=[
            pltpu.VMEM((3 * nb + CTX_LEN, kvw), BF16),
            pltpu.VMEM((ATTN_KV_HEADS, 2 * (3 * nb + CTX_LEN), LANES), BF16),
        ],
        compiler_params=pltpu.CompilerParams(
            dimension_semantics=("parallel", "parallel"), vmem_limit_bytes=VMEM_LIMIT),
        name="attn_core",
    )(sink, q, k, k, k, k, v, v, v, v, g, *stream, mods, w_out)


def _ssd_in_kernel(h_ref, hp_ref, hn_ref, mod_ref, w_ref, cw_ref, cb_ref, dtb_ref,
                   z_ref, xs_ref, bm_ref, cm_ref, dt_ref):
    blk = pl.program_id(1)
    n_tiles = pl.num_programs(1)
    ctx_tiles = CTX_LEN // ROW_TILE
    halo = SUBLANES
    tm = ROW_TILE
    width = xs_ref.shape[-1]
    gs_width = bm_ref.shape[-1]

    prev_ok = (blk > ctx_tiles) | ((blk > 0) & (blk < ctx_tiles))
    next_ok = ((blk >= ctx_tiles) & (blk < n_tiles - 1)) | (blk < ctx_tiles - 1)
    u_main = _norm_modulate(h_ref[...], mod_ref)
    u_all = jnp.concatenate(
        [_norm_modulate(hp_ref[...], mod_ref) * prev_ok.astype(F32), u_main,
         _norm_modulate(hn_ref[...], mod_ref) * next_ok.astype(F32)], axis=0).astype(BF16)
    ub = u_main.astype(BF16)

    conv_dim = cw_ref.shape[-1]
    dt_raw = jnp.dot(ub, w_ref[:, width + conv_dim:], preferred_element_type=F32) + dtb_ref[...]
    dt_ref[...] = jnp.maximum(dt_raw, 0.0) + jnp.log1p(jnp.exp(-jnp.abs(dt_raw)))
    z_tile = width // (conv_dim // CONV_COL_TILE)

    pad = SSD_CONV // 2
    n_rows = tm + 2 * halo
    sub = lax.broadcasted_iota(jnp.int32, (1, SUBLANES, CONV_COL_TILE), 1)
    for c in range(conv_dim // CONV_COL_TILE):
        cs = slice(c * CONV_COL_TILE, (c + 1) * CONV_COL_TILE)
        pre = jnp.dot(u_all, w_ref[:, width + c * CONV_COL_TILE:width + (c + 1) * CONV_COL_TILE],
                      preferred_element_type=F32)
        tiles = pre.reshape(n_rows // SUBLANES, SUBLANES, CONV_COL_TILE)
        n_out = tm // SUBLANES
        own = tiles[1:1 + n_out]
        acc = cb_ref[:, cs] + cw_ref[pad:pad + 1, cs] * own
        for kk in range(SSD_CONV):
            off = kk - pad
            if off == 0:
                continue
            from_own = (sub >= off) if off > 0 else (sub < SUBLANES + off)
            nbr = 1 + (1 if off > 0 else -1)
            mixed = jnp.where(from_own, own, tiles[nbr:nbr + n_out])
            acc = acc + cw_ref[kk:kk + 1, cs] * pltpu.roll(mixed, (-off) % SUBLANES, 1)
        out = _silu(acc).reshape(tm, CONV_COL_TILE).astype(BF16)
        zs = slice(c * z_tile, (c + 1) * z_tile)
        z_ref[:, zs] = jnp.dot(ub, w_ref[:, zs], preferred_element_type=F32).astype(BF16)
        lo = c * CONV_COL_TILE
        if lo < width:
            xs_ref[:, lo:lo + CONV_COL_TILE] = out
        elif lo < width + gs_width:
            bm_ref[:, lo - width:lo - width + CONV_COL_TILE] = out
        else:
            cm_ref[:, lo - width - gs_width:lo - width - gs_width + CONV_COL_TILE] = out


def _ssd_in(hs, mods, layer, w_in, conv_w, conv_b, dt_bias):
    bsz, t, d = hs.shape
    width = w_in.shape[1] - conv_w.shape[1] - LANES
    gs_width = SSD_GROUPS * SSD_STATE
    n_tiles = t // ROW_TILE
    ctx_tiles = CTX_LEN // ROW_TILE
    per_tile = ROW_TILE // SUBLANES
    n_halo_blocks = t // SUBLANES
    row_spec = lambda w_: pl.BlockSpec((None, ROW_TILE, w_), lambda b, i: (b, i, 0))
    full = lambda a: pl.BlockSpec(a.shape, lambda b, i: (0,) * a.ndim)
    return pl.pallas_call(
        _ssd_in_kernel,
        grid=(bsz, n_tiles),
        in_specs=[
            row_spec(d),
            pl.BlockSpec((None, SUBLANES, d), lambda b, i: (b, jnp.maximum(i * per_tile - 1, 0), 0)),
            pl.BlockSpec((None, SUBLANES, d),
                         lambda b, i: (b, jnp.minimum((i + 1) * per_tile, n_halo_blocks - 1), 0)),
            pl.BlockSpec((None, None, 3, d), lambda b, i: (layer, _mod_row(i < ctx_tiles, b), 0, 0)),
            full(w_in), full(conv_w), full(conv_b), full(dt_bias),
        ],
        out_specs=[row_spec(width), row_spec(width), row_spec(gs_width), row_spec(gs_width),
                   row_spec(LANES)],
        out_shape=[
            jax.ShapeDtypeStruct((bsz, t, width), BF16),
            jax.ShapeDtypeStruct((bsz, t, width), BF16),
            jax.ShapeDtypeStruct((bsz, t, gs_width), BF16),
            jax.ShapeDtypeStruct((bsz, t, gs_width), BF16),
            jax.ShapeDtypeStruct((bsz, t, LANES), F32),
        ],
        compiler_params=pltpu.CompilerParams(
            dimension_semantics=("parallel", "parallel"), vmem_limit_bytes=VMEM_LIMIT),
        name="ssd_in",
    )(hs, hs, hs, mods, w_in, conv_w, conv_b, dt_bias)


def _ssd_chunk(direction, xs_ref, bm_ref, cm_ref, dt_ref, alog_ref, st_ref, emit, skip_ref=None,
               issue_ahead=False):
    q = SSD_CHUNK
    log2e = math.log2(math.e)
    dt = dt_ref[...]
    da = dt * (-jnp.exp(alog_ref[...]))
    row = lax.broadcasted_iota(jnp.int32, (q, q), 0)
    col = lax.broadcasted_iota(jnp.int32, (q, q), 1)
    seen = (row >= col) if direction == 0 else (col >= row)
    a = jnp.dot(seen.astype(F32), da, preferred_element_type=F32,
                precision=lax.Precision.HIGHEST)
    last = q - 1 if direction == 0 else 0
    a_tot = a[last:last + 1, :]
    w_end = jnp.exp(a_tot - a) * dt
    dec = jnp.exp(a_tot)
    a2 = a * log2e
    src2_t = ((a - jnp.log(dt)) * log2e).T
    left = col < HEAD_DIM

    def group_products(g):
        gsl = slice(g * SSD_STATE, (g + 1) * SSD_STATE)
        cmg = cm_ref[:, gsl]
        bmg = bm_ref[:, gsl]
        cb = lax.dot_general(cmg, bmg, (((1,), (1,)), ((), ())), preferred_element_type=F32)
        h_in = st_ref[g]
        y_off = jnp.dot(cmg, h_in.astype(BF16), preferred_element_type=F32)
        return bmg, cb, h_in, y_off

    ahead = group_products(0) if issue_ahead else None
    for g in range(SSD_GROUPS):
        bmg, cb, h_in, y_off = ahead if issue_ahead else group_products(g)
        if issue_ahead and g + 1 < SSD_GROUPS:
            ahead = group_products(g + 1)
        ys, xws, decs = [], [], []
        for pr in range(SSD_HEADS_PER_GROUP // 2):
            pair = g * (SSD_HEADS_PER_GROUP // 2) + pr
            c0 = direction * SSD_HEADS + 2 * pair
            psl = slice(pair * LANES, (pair + 1) * LANES)
            xp = xs_ref[:, psl]
            ws, eas = [], []
            for c in (c0, c0 + 1):
                a_col = jnp.broadcast_to(a2[:, c:c + 1], (q, q))
                lmat = jnp.exp2(jnp.where(seen, a_col - src2_t[c:c + 1, :], -jnp.inf))
                ws.append((cb * lmat).astype(BF16))
                eas.append(jnp.exp2(a_col))
            zero = jnp.zeros_like(xp)
            x_diag = jnp.concatenate([jnp.where(left, xp, zero), jnp.where(left, zero, xp)], axis=0)
            y_diag = jnp.dot(jnp.concatenate(ws, axis=1), x_diag, preferred_element_type=F32)
            y = y_diag + y_off[:, pr * LANES:(pr + 1) * LANES] * jnp.where(left, eas[0], eas[1])
            xf = xp.astype(F32)
            if skip_ref is not None:
                y = y + skip_ref[:, psl] * xf
            ys.append(y)
            idx = jnp.where(left, c0, c0 + 1)
            xws.append((xf * jnp.take_along_axis(w_end, idx, axis=1)).astype(BF16))
            decs.append(jnp.where(left[0:1, :], jnp.broadcast_to(dec[:, c0:c0 + 1], (1, LANES)),
                                  jnp.broadcast_to(dec[:, c0 + 1:c0 + 2], (1, LANES))))
        emit(g, jnp.concatenate(ys, axis=1))
        new = lax.dot_general(bmg, jnp.concatenate(xws, axis=1), (((0,), (0,)), ((), ())),
                              preferred_element_type=F32)
        st_ref[g] = h_in * jnp.concatenate(decs, axis=1) + new


def _ssd_fwd_kernel(xs_ref, bm_ref, cm_ref, dt_ref, alog_ref, dsk_ref, y_ref, st_ref):
    @pl.when(pl.program_id(1) == 0)
    def _():
        st_ref[...] = jnp.zeros_like(st_ref)

    def emit(g, y):
        y_ref[:, g * SSD_GROUP_WIDTH:(g + 1) * SSD_GROUP_WIDTH] = y

    _ssd_chunk(0, xs_ref, bm_ref, cm_ref, dt_ref, alog_ref, st_ref, emit, skip_ref=dsk_ref,
               issue_ahead=True)


def _ssd_bwd_kernel(final, xs_ref, bm_ref, cm_ref, dt_ref, alog_ref, yf_ref, z_ref, h_ref, mod_ref,
                    nw_ref, wo_ref, fw_ref, o_ref, st_ref, u_ref):
    step = pl.program_id(1)

    @pl.when(step == 0)
    def _():
        st_ref[...] = jnp.zeros_like(st_ref)
        u_ref[...] = jnp.zeros_like(u_ref)

    y = jnp.dot(u_ref[(step + 1) % 2], wo_ref[...], preferred_element_type=F32)
    h = h_ref[...] + mod_ref[2:3, :] * y
    o_ref[...] = _rms_normalize(h) * fw_ref[...] if final else h

    def emit(g, y):
        sl = slice(g * SSD_GROUP_WIDTH, (g + 1) * SSD_GROUP_WIDTH)
        u = (y + yf_ref[:, sl]) * _silu(z_ref[:, sl].astype(F32))
        u_ref[step % 2, :, sl] = (_rms_normalize(u) * nw_ref[:, sl]).astype(BF16)

    _ssd_chunk(1, xs_ref, bm_ref, cm_ref, dt_ref, alog_ref, st_ref, emit)


def _ssd_scans(hs, mods, layer, z, xs, bm, cm, dt, a_log, d_skip, norm_w, w_out, final_w):
    bsz, t, d = hs.shape
    width = xs.shape[-1]
    gs_width = bm.shape[-1]
    q = SSD_CHUNK
    n_chunks = t // q
    ctx_chunks = CTX_LEN // q
    fwd = lambda b, i: (b, i, 0)
    bwd_chunk = lambda i: jnp.where(i < ctx_chunks, ctx_chunks - 1 - i, n_chunks - 1 + ctx_chunks - i)
    const = lambda a: pl.BlockSpec(a.shape, lambda b, i: (0,) * a.ndim)
    state = pltpu.VMEM((SSD_GROUPS, SSD_STATE, SSD_GROUP_WIDTH), F32)
    chunk_specs = lambda imap: [
        pl.BlockSpec((None, q, width), imap), pl.BlockSpec((None, q, gs_width), imap),
        pl.BlockSpec((None, q, gs_width), imap), pl.BlockSpec((None, q, LANES), imap)]

    y_f = pl.pallas_call(
        _ssd_fwd_kernel,
        grid=(bsz, n_chunks),
        in_specs=[*chunk_specs(fwd), const(a_log), const(d_skip)],
        out_specs=pl.BlockSpec((None, q, width), fwd),
        out_shape=jax.ShapeDtypeStruct((bsz, t, width), F32),
        scratch_shapes=[state],
        compiler_params=pltpu.CompilerParams(
            dimension_semantics=("parallel", "arbitrary"), vmem_limit_bytes=VMEM_LIMIT),
        name="ssd_fwd",
    )(xs, bm, cm, dt, a_log, d_skip)

    scanned = lambda b, i: (b, bwd_chunk(jnp.minimum(i, n_chunks - 1)), 0)
    behind = lambda i: bwd_chunk(jnp.maximum(i - 1, 0))
    final = final_w is not None
    if final:
        out_rows = t - CTX_LEN
        out_map = lambda b, i: (b, bwd_chunk(jnp.maximum(i - 1, ctx_chunks)) - ctx_chunks, 0)
    else:
        final_w = jnp.zeros((1, d), F32)
        out_rows = t
        out_map = lambda b, i: (b, behind(i), 0)
    return pl.pallas_call(
        functools.partial(_ssd_bwd_kernel, final),
        grid=(bsz, n_chunks + 1),
        in_specs=[
            *chunk_specs(scanned), const(a_log),
            pl.BlockSpec((None, q, width), scanned),
            pl.BlockSpec((None, q, width), scanned),
            pl.BlockSpec((None, q, d), lambda b, i: (b, behind(i), 0)),
            pl.BlockSpec((None, None, 3, d),
                         lambda b, i: (layer, _mod_row(behind(i) < ctx_chunks, b), 0, 0)),
            const(norm_w), const(w_out), const(final_w),
        ],
        out_specs=pl.BlockSpec((None, q, d), out_map),
        out_shape=jax.ShapeDtypeStruct((bsz, out_rows, d), F32),
        scratch_shapes=[state, pltpu.VMEM((2, q, width), BF16)],
        compiler_params=pltpu.CompilerParams(
            dimension_semantics=("parallel", "arbitrary"), vmem_limit_bytes=VMEM_LIMIT),
        name="ssd_bwd",
    )(xs, bm, cm, dt, a_log, y_f, z, hs, mods, norm_w, w_out, final_w)


def _rope_tables(seq_len):
    quarter = HEAD_DIM // 4
    pos = jnp.arange(seq_len, dtype=jnp.int32)
    row = (pos // GRID_W).astype(F32)
    col = (pos % GRID_W).astype(F32)
    inv_freq = ROPE_BASE ** (-jnp.arange(0, 2 * quarter, 2, dtype=F32) / (2 * quarter))
    lane = jnp.arange(LANES)
    in_head = lane % HEAD_DIM
    p = jnp.where((in_head < HEAD_DIM // 2)[None, :], row[:, None], col[:, None])
    ang = p * inv_freq[lane % quarter][None, :]
    first = ((lane % (2 * quarter)) < quarter)[None, :]
    cos = jnp.cos(ang)
    sin = jnp.sin(ang)
    sa = jnp.where(first, -sin, 0.0)
    sb = jnp.where(first, 0.0, sin)
    ident = jnp.ones((CTX_LEN, LANES), F32)
    zeros = jnp.zeros((CTX_LEN, LANES), F32)
    return (jnp.concatenate([ident, cos], 0), jnp.concatenate([zeros, sa], 0),
            jnp.concatenate([zeros, sb], 0))


def kernel(x, c, ctx, c_ctx, w_ada, b_ada, attn_w_in, attn_sink, attn_w_out, ssd_w_in, ssd_conv_w,
           ssd_conv_b, ssd_dt_bias, ssd_a_log, ssd_d, ssd_norm_w, ssd_w_out, final_norm_w):
    bsz, seq_len, d = x.shape
    depth = w_ada.shape[0]
    assert ctx.shape[1] == CTX_LEN and CTX_LEN % ROW_TILE == 0 and seq_len % ROW_TILE == 0
    assert bsz <= 8 and depth % 2 == 0

    cond = jnp.zeros((16, d), F32).at[:bsz].set(c).at[8].set(c_ctx)
    mods = _modulation_tables(cond, w_ada, b_ada).reshape(depth, 16, 3, d)
    cos, sa, sb = _rope_tables(seq_len)
    hs = (ctx, x)

    pad_lanes = lambda a: jnp.pad(a, ((0, 0), (0, LANES - a.shape[-1])))

    for i in range(depth):
        j = i // 2
        if i % 2 == 0:
            q, k, v, g = _attn_in(hs, mods, i, attn_w_in[j].astype(BF16), cos, sa, sb)
            hs = _attn_core(hs, mods, i, q, k, v, g, attn_sink[j], attn_w_out[j].astype(BF16))
        else:
            w_in = jnp.pad(ssd_w_in[j], ((0, 0), (0, LANES - 2 * SSD_HEADS))).astype(BF16)
            z, xs, bm, cm, dt = _ssd_in(
                hs, mods, i, w_in, ssd_conv_w[j], ssd_conv_b[j][None, :],
                pad_lanes(ssd_dt_bias[j].reshape(1, -1)))
            hs = _ssd_scans(
                hs, mods, i, z, xs, bm, cm, dt, pad_lanes(ssd_a_log[j].reshape(1, -1)),
                jnp.repeat(ssd_d[j], HEAD_DIM)[None, :], ssd_norm_w[j][None, :],
                ssd_w_out[j].astype(BF16), final_norm_w[None, :] if i == depth - 1 else None)
    return hs
```

```python
import functools
import math

import jax
import jax.numpy as jnp
from jax import lax
from jax.experimental import pallas as pl
from jax.experimental.pallas import tpu as pltpu

F32 = jnp.float32
BF16 = jnp.bfloat16

EPS = 1e-6
CTX_LEN = 256
GRID_W = 64
ROPE_BASE = 10000.0

HEAD_DIM = 64
ATTN_HEADS = 16
ATTN_KV_HEADS = 4
ATTN_GROUP = ATTN_HEADS // ATTN_KV_HEADS
ATTN_BLOCK = 128
LANES = 128
SUBLANES = 8

SSD_HEADS = 32
SSD_GROUPS = 8
SSD_HEADS_PER_GROUP = SSD_HEADS // SSD_GROUPS
SSD_STATE = 128
SSD_CONV = 5
SSD_CHUNK = 128
SSD_GROUP_WIDTH = SSD_HEADS_PER_GROUP * HEAD_DIM

ROW_TILE = 256
CONV_COL_TILE = 512
MASKED = -1e30
VMEM_LIMIT = 56 * 1024 * 1024


def _silu(x):
    return x / (1.0 + jnp.exp2(x * -math.log2(math.e)))


def _rms_normalize(x):
    ms = jnp.mean(x * x, axis=-1, keepdims=True)
    return x * lax.rsqrt(ms + EPS)


def _norm_modulate(x, mod_ref):
    return _rms_normalize(x) * (1.0 + mod_ref[1:2, :]) + mod_ref[0:1, :]


def _mod_row(is_ctx, b):
    return jnp.where(is_ctx, 8, b)


def _mod_kernel(cond_ref, w_ref, b_ref, o_ref):
    cnd = cond_ref[...]
    o_ref[...] = jnp.dot(_silu(cnd), w_ref[...], preferred_element_type=F32,
                         precision=lax.Precision.HIGHEST) + b_ref[...]


def _modulation_tables(cond, w_ada, b_ada):
    depth, d, d3 = w_ada.shape
    return pl.pallas_call(
        _mod_kernel,
        grid=(depth, d3 // d),
        in_specs=[
            pl.BlockSpec((16, d), lambda l, j: (0, 0)),
            pl.BlockSpec((None, d, d), lambda l, j: (l, 0, j)),
            pl.BlockSpec((None, 1, d), lambda l, j: (l, 0, j)),
        ],
        out_specs=pl.BlockSpec((None, 16, d), lambda l, j: (l, 0, j)),
        out_shape=jax.ShapeDtypeStruct((depth, 16, d3), F32),
        compiler_params=pltpu.CompilerParams(vmem_limit_bytes=VMEM_LIMIT),
        name="adaln_tables",
    )(cond, w_ada, b_ada.reshape(depth, 1, d3))


def _attn_in_kernel(split, *refs):
    if split:
        c_ref, h_ref, *refs = refs
        h = jnp.where(pl.program_id(1) < CTX_LEN // ROW_TILE, c_ref[...], h_ref[...])
    else:
        h_ref, *refs = refs
        h = h_ref[...]
    mod_ref, w_ref, cos_ref, sa_ref, sb_ref, q_ref, k_ref, v_ref, g_ref = refs
    d = h_ref.shape[-1]
    kvw = ATTN_KV_HEADS * HEAD_DIM
    ub = _norm_modulate(h, mod_ref).astype(BF16)
    cos = cos_ref[...]
    sa = sa_ref[...]
    sb = sb_ref[...]

    def rope(t):
        return t * cos + pltpu.roll(t, LANES - 16, 1) * sa + pltpu.roll(t, 16, 1) * sb

    left = lax.broadcasted_iota(jnp.int32, (h_ref.shape[0], LANES), 1) < HEAD_DIM

    def store_doubled(ref, j, t):
        sw = pltpu.roll(t, HEAD_DIM, 1)
        ref[:, (2 * j) * LANES:(2 * j + 1) * LANES] = jnp.where(left, t, sw).astype(BF16)
        ref[:, (2 * j + 1) * LANES:(2 * j + 2) * LANES] = jnp.where(left, sw, t).astype(BF16)

    q = jnp.dot(ub, w_ref[:, 0:d], preferred_element_type=F32)
    scale = HEAD_DIM ** -0.5 * math.log2(math.e)
    for j in range(d // LANES):
        sl = slice(j * LANES, (j + 1) * LANES)
        q_ref[:, sl] = (rope(q[:, sl]) * scale).astype(BF16)
    k = jnp.dot(ub, w_ref[:, d:d + kvw], preferred_element_type=F32)
    v = jnp.dot(ub, w_ref[:, d + kvw:d + 2 * kvw], preferred_element_type=F32)
    for j in range(kvw // LANES):
        sl = slice(j * LANES, (j + 1) * LANES)
        store_doubled(k_ref, j, rope(k[:, sl]))
        store_doubled(v_ref, j, v[:, sl])
    g = jnp.dot(ub, w_ref[:, d + 2 * kvw:], preferred_element_type=F32)
    g_ref[...] = _silu(g).astype(BF16)


def _stream_specs(hs, rows):
    if not isinstance(hs, tuple):
        return [pl.BlockSpec((None, rows, hs.shape[-1]), lambda b, i: (b, i, 0))], [hs], hs.shape[1]
    ctx, x = hs
    n_ctx = CTX_LEN // rows
    d = x.shape[-1]
    return ([pl.BlockSpec((None, rows, d), lambda b, i: (b, jnp.minimum(i, n_ctx - 1), 0)),
             pl.BlockSpec((None, rows, d), lambda b, i: (b, jnp.maximum(i - n_ctx, 0), 0))],
            [ctx, x], CTX_LEN + x.shape[1])


def _attn_in(hs, mods, layer, w_in, cos, sa, sb):
    stream_specs, stream, t = _stream_specs(hs, ROW_TILE)
    bsz, _, d = stream[0].shape
    kvw = ATTN_KV_HEADS * HEAD_DIM
    n_in = w_in.shape[1]
    ctx_tiles = CTX_LEN // ROW_TILE
    row_spec = lambda width: pl.BlockSpec((None, ROW_TILE, width), lambda b, i: (b, i, 0))
    tab_spec = pl.BlockSpec((ROW_TILE, LANES), lambda b, i: (i, 0))
    return pl.pallas_call(
        functools.partial(_attn_in_kernel, len(stream) == 2),
        grid=(bsz, t // ROW_TILE),
        in_specs=[
            *stream_specs,
            pl.BlockSpec((None, None, 3, d), lambda b, i: (layer, _mod_row(i < ctx_tiles, b), 0, 0)),
            pl.BlockSpec((d, n_in), lambda b, i: (0, 0)),
            tab_spec, tab_spec, tab_spec,
        ],
        out_specs=[row_spec(d), row_spec(2 * kvw), row_spec(2 * kvw), row_spec(d)],
        out_shape=[
            jax.ShapeDtypeStruct((bsz, t, d), BF16),
            jax.ShapeDtypeStruct((bsz, t, 2 * kvw), BF16),
            jax.ShapeDtypeStruct((bsz, t, 2 * kvw), BF16),
            jax.ShapeDtypeStruct((bsz, t, d), BF16),
        ],
        compiler_params=pltpu.CompilerParams(
            dimension_semantics=("parallel", "parallel"), vmem_limit_bytes=VMEM_LIMIT),
        name="attn_in",
    )(*stream, mods, w_in, cos, sa, sb)


def _attn_core_kernel(n_lat_blocks, split, sink_ref, q_ref, kp_ref, kc_ref, kn_ref, kx_ref,
                      vp_ref, vc_ref, vn_ref, vx_ref, g_ref, *refs):
    if split:
        c_ref, h_ref, mod_ref, wo_ref, o_ref, kcat, vdiag = refs
    else:
        h_ref, mod_ref, wo_ref, o_ref, kcat, vdiag = refs
        c_ref = h_ref
    blk = pl.program_id(1)
    ctx_blocks = CTX_LEN // ATTN_BLOCK
    j = blk - ctx_blocks
    nb = ATTN_BLOCK
    pairs = ATTN_GROUP // 2
    log2e = math.log2(math.e)
    nt_dims = (((1,), (1,)), ((), ()))

    def left(rows):
        return lax.broadcasted_iota(jnp.int32, (rows, LANES), 1) < HEAD_DIM

    def split_heads(t):
        zero = jnp.zeros_like(t)
        return jnp.where(left(t.shape[0]), t, zero), jnp.where(left(t.shape[0]), zero, t)

    def stacked_queries(kh):
        rows = []
        for pr in range(pairs):
            rows.extend(split_heads(q_ref[:, (kh * pairs + pr) * LANES:(kh * pairs + pr + 1) * LANES]))
        return jnp.concatenate(rows, axis=0)

    def softmax(kh, s):
        sink = jnp.concatenate(
            [jnp.full((nb, 1), sink_ref[kh * ATTN_GROUP + r] * log2e, F32) for r in range(ATTN_GROUP)],
            axis=0)
        m = jnp.maximum(jnp.max(s, axis=-1, keepdims=True), sink)
        p = jnp.exp2(s - m)
        den = jnp.sum(p, axis=-1, keepdims=True) + jnp.exp2(sink - m)
        return p.astype(BF16), 1.0 / den

    def weighted_values(kh, pb, r, n_keys):
        lhs = jnp.concatenate(
            [jnp.concatenate([pb[2 * pr * nb:(2 * pr + 1) * nb], pb[(2 * pr + 1) * nb:(2 * pr + 2) * nb]],
                             axis=1) for pr in range(pairs)], axis=0)
        o = jnp.dot(lhs, vdiag[kh, 0:2 * n_keys, :], preferred_element_type=F32)
        gated = []
        for pr in range(pairs):
            lo, mid, hi = 2 * pr * nb, (2 * pr + 1) * nb, (2 * pr + 2) * nb
            rr = jnp.where(left(nb), r[lo:mid], r[mid:hi])
            sl = slice((kh * pairs + pr) * LANES, (kh * pairs + pr + 1) * LANES)
            gated.append((o[pr * nb:(pr + 1) * nb] * rr * g_ref[:, sl].astype(F32)).astype(BF16))
        rows = slice(kh * pairs * LANES, (kh + 1) * pairs * LANES)
        return jnp.dot(jnp.concatenate(gated, axis=1), wo_ref[rows, :], preferred_element_type=F32)

    def attend(key_blocks, bias, res_ref):
        n_keys = sum(k_ref.shape[0] for k_ref, _ in key_blocks)
        r0 = 0
        for k_ref, v_ref in key_blocks:
            r1 = r0 + k_ref.shape[0]
            kcat[r0:r1, :] = k_ref[...]
            for kh in range(ATTN_KV_HEADS):
                top, bot = split_heads(v_ref[:, kh * LANES:(kh + 1) * LANES])
                vdiag[kh, r0:r1, :] = top
                vdiag[kh, n_keys + r0:n_keys + r1, :] = bot
            r0 = r1

        def logits(kh):
            s = lax.dot_general(stacked_queries(kh), kcat[0:n_keys, kh * LANES:(kh + 1) * LANES],
                                nt_dims, preferred_element_type=F32)
            if not bias:
                return s
            return jnp.concatenate(
                [s[:, i * nb:(i + 1) * nb] + bias[i] if i in bias else s[:, i * nb:(i + 1) * nb]
                 for i in range(n_keys // nb)], axis=1)

        s_next = logits(0)
        pending = None
        y = None
        for kh in range(ATTN_KV_HEADS):
            s = s_next
            if kh + 1 < ATTN_KV_HEADS:
                s_next = logits(kh + 1)
            pb, r = softmax(kh, s)
            if pending is not None:
                part = weighted_values(*pending, n_keys)
                y = part if y is None else y + part
            pending = (kh, pb, r)
        y = y + weighted_values(*pending, n_keys)
        o_ref[...] = res_ref[...] + mod_ref[2:3, :] * y

    @pl.when(blk >= ctx_blocks)
    def _():
        row = lax.broadcasted_iota(jnp.int32, (nb, nb), 0)
        col = lax.broadcasted_iota(jnp.int32, (nb, nb), 1)
        bias_prev = jnp.where((col >= row) & (j > 0), 0.0, MASKED).astype(F32)
        bias_next = jnp.where((col <= row) & (j < n_lat_blocks - 1), 0.0, MASKED).astype(F32)
        attend(((kp_ref, vp_ref), (kc_ref, vc_ref), (kn_ref, vn_ref), (kx_ref, vx_ref)),
               {0: jnp.concatenate([bias_prev] * ATTN_GROUP, axis=0),
                2: jnp.concatenate([bias_next] * ATTN_GROUP, axis=0)}, h_ref)

    @pl.when(blk < ctx_blocks)
    def _():
        attend(((kx_ref, vx_ref),), {}, c_ref)


def _attn_core(hs, mods, layer, q, k, v, g, sink, w_out):
    nb = ATTN_BLOCK
    stream_specs, stream, t = _stream_specs(hs, nb)
    bsz, _, d = stream[0].shape
    kvw = ATTN_KV_HEADS * LANES
    ctx_blocks = CTX_LEN // nb
    n_blocks = t // nb
    n_lat = n_blocks - ctx_blocks
    blk_spec = lambda width, imap: pl.BlockSpec((None, nb, width), imap)
    own = lambda b, i: (b, i, 0)
    prev = lambda b, i: (b, jnp.maximum(i - 1, ctx_blocks), 0)
    nxt = lambda b, i: (b, jnp.minimum(jnp.maximum(i + 1, ctx_blocks), n_blocks - 1), 0)
    ctx_spec = pl.BlockSpec((None, CTX_LEN, kvw), lambda b, i: (b, 0, 0))
    kv_specs = [blk_spec(kvw, prev), blk_spec(kvw, own), blk_spec(kvw, nxt), ctx_spec]
    return pl.pallas_call(
        functools.partial(_attn_core_kernel, n_lat, len(stream) == 2),
        grid=(bsz, n_blocks),
        in_specs=[
            pl.BlockSpec(memory_space=pltpu.SMEM),
            blk_spec(d, own),
            *kv_specs, *kv_specs,
            blk_spec(d, own),
            *stream_specs,
            pl.BlockSpec((None, None, 3, d), lambda b, i: (layer, _mod_row(i < ctx_blocks, b), 0, 0)),
            pl.BlockSpec(w_out.shape, lambda b, i: (0, 0)),
        ],
        out_specs=blk_spec(d, own),
        out_shape=jax.ShapeDtypeStruct((bsz, t, d), F32),
        scratch_shapes=[
            pltpu.VMEM((3 * nb + CTX_LEN, kvw), BF16),
            pltpu.VMEM((ATTN_KV_HEADS, 2 * (3 * nb + CTX_LEN), LANES), BF16),
        ],
        compiler_params=pltpu.CompilerParams(
            dimension_semantics=("parallel", "parallel"), vmem_limit_bytes=VMEM_LIMIT),
        name="attn_core",
    )(sink, q, k, k, k, k, v, v, v, v, g, *stream, mods, w_out)


def _ssd_in_kernel(h_ref, hp_ref, hn_ref, mod_ref, w_ref, cw_ref, cb_ref, dtb_ref, alog_ref,
                   z_ref, xs_ref, bm_ref, cm_ref, coef_ref):
    blk = pl.program_id(1)
    n_tiles = pl.num_programs(1)
    ctx_tiles = CTX_LEN // ROW_TILE
    halo = SUBLANES
    tm = ROW_TILE
    width = xs_ref.shape[-1]
    gs_width = bm_ref.shape[-1]

    prev_ok = (blk > ctx_tiles) | ((blk > 0) & (blk < ctx_tiles))
    next_ok = ((blk >= ctx_tiles) & (blk < n_tiles - 1)) | (blk < ctx_tiles - 1)
    u_main = _norm_modulate(h_ref[...], mod_ref)
    u_all = jnp.concatenate(
        [_norm_modulate(hp_ref[...], mod_ref) * prev_ok.astype(F32), u_main,
         _norm_modulate(hn_ref[...], mod_ref) * next_ok.astype(F32)], axis=0).astype(BF16)
    ub = u_main.astype(BF16)

    conv_dim = cw_ref.shape[-1]
    dt_raw = jnp.dot(ub, w_ref[:, width + conv_dim:], preferred_element_type=F32) + dtb_ref[...]
    dt = jnp.maximum(dt_raw, 0.0) + jnp.log1p(jnp.exp(-jnp.abs(dt_raw)))
    neg_a = -jnp.exp(alog_ref[...])
    for k in range(tm // SSD_CHUNK):
        rows = slice(k * SSD_CHUNK, (k + 1) * SSD_CHUNK)
        coef_ref[rows, :] = _scan_coefficients(dt[rows], neg_a)
    z_tile = width // (conv_dim // CONV_COL_TILE)

    pad = SSD_CONV // 2
    n_rows = tm + 2 * halo
    sub = lax.broadcasted_iota(jnp.int32, (1, SUBLANES, CONV_COL_TILE), 1)
    for c in range(conv_dim // CONV_COL_TILE):
        cs = slice(c * CONV_COL_TILE, (c + 1) * CONV_COL_TILE)
        pre = jnp.dot(u_all, w_ref[:, width + c * CONV_COL_TILE:width + (c + 1) * CONV_COL_TILE],
                      preferred_element_type=F32)
        tiles = pre.reshape(n_rows // SUBLANES, SUBLANES, CONV_COL_TILE)
        n_out = tm // SUBLANES
        own = tiles[1:1 + n_out]
        acc = cb_ref[:, cs] + cw_ref[pad:pad + 1, cs] * own
        for kk in range(SSD_CONV):
            off = kk - pad
            if off == 0:
                continue
            from_own = (sub >= off) if off > 0 else (sub < SUBLANES + off)
            nbr = 1 + (1 if off > 0 else -1)
            mixed = jnp.where(from_own, own, tiles[nbr:nbr + n_out])
            acc = acc + cw_ref[kk:kk + 1, cs] * pltpu.roll(mixed, (-off) % SUBLANES, 1)
        out = _silu(acc).reshape(tm, CONV_COL_TILE).astype(BF16)
        zs = slice(c * z_tile, (c + 1) * z_tile)
        z_ref[:, zs] = jnp.dot(ub, w_ref[:, zs], preferred_element_type=F32).astype(BF16)
        lo = c * CONV_COL_TILE
        if lo < width:
            xs_ref[:, lo:lo + CONV_COL_TILE] = out
        elif lo < width + gs_width:
            bm_ref[:, lo - width:lo - width + CONV_COL_TILE] = out
        else:
            cm_ref[:, lo - width - gs_width:lo - width - gs_width + CONV_COL_TILE] = out


def _ssd_in(hs, mods, layer, w_in, conv_w, conv_b, dt_bias, a_log):
    bsz, t, d = hs.shape
    width = w_in.shape[1] - conv_w.shape[1] - LANES
    gs_width = SSD_GROUPS * SSD_STATE
    n_tiles = t // ROW_TILE
    ctx_tiles = CTX_LEN // ROW_TILE
    per_tile = ROW_TILE // SUBLANES
    n_halo_blocks = t // SUBLANES
    row_spec = lambda w_: pl.BlockSpec((None, ROW_TILE, w_), lambda b, i: (b, i, 0))
    full = lambda a: pl.BlockSpec(a.shape, lambda b, i: (0,) * a.ndim)
    return pl.pallas_call(
        _ssd_in_kernel,
        grid=(bsz, n_tiles),
        in_specs=[
            row_spec(d),
            pl.BlockSpec((None, SUBLANES, d), lambda b, i: (b, jnp.maximum(i * per_tile - 1, 0), 0)),
            pl.BlockSpec((None, SUBLANES, d),
                         lambda b, i: (b, jnp.minimum((i + 1) * per_tile, n_halo_blocks - 1), 0)),
            pl.BlockSpec((None, None, 3, d), lambda b, i: (layer, _mod_row(i < ctx_tiles, b), 0, 0)),
            full(w_in), full(conv_w), full(conv_b), full(dt_bias), full(a_log),
        ],
        out_specs=[row_spec(width), row_spec(width), row_spec(gs_width), row_spec(gs_width),
                   row_spec(4 * LANES)],
        out_shape=[
            jax.ShapeDtypeStruct((bsz, t, width), BF16),
            jax.ShapeDtypeStruct((bsz, t, width), BF16),
            jax.ShapeDtypeStruct((bsz, t, gs_width), BF16),
            jax.ShapeDtypeStruct((bsz, t, gs_width), BF16),
            jax.ShapeDtypeStruct((bsz, t, 4 * LANES), F32),
        ],
        compiler_params=pltpu.CompilerParams(
            dimension_semantics=("parallel", "parallel"), vmem_limit_bytes=VMEM_LIMIT),
        name="ssd_in",
    )(hs, hs, hs, mods, w_in, conv_w, conv_b, dt_bias, a_log)


def _scan_coefficients(dt, neg_a):
    q = SSD_CHUNK
    log2e = math.log2(math.e)
    row = lax.broadcasted_iota(jnp.int32, (q, q), 0)
    col = lax.broadcasted_iota(jnp.int32, (q, q), 1)
    both = jnp.concatenate([row >= col, col >= row], axis=0).astype(F32)
    cum = jnp.dot(both, dt * neg_a, preferred_element_type=F32, precision=lax.Precision.HIGHEST)
    fwd_lane = lax.broadcasted_iota(jnp.int32, (q, LANES), 1) < SSD_HEADS
    a = jnp.where(fwd_lane, cum[0:q], cum[q:2 * q])
    a_tot = jnp.where(fwd_lane[0:1], cum[q - 1:q], cum[q:q + 1])
    return jnp.concatenate(
        [a * log2e, ((a - jnp.log(dt)) * log2e).T, jnp.exp(a_tot - a) * dt,
         jnp.broadcast_to(jnp.exp(a_tot), (q, LANES))], axis=1)


def _ssd_chunk(direction, xs_ref, bm_ref, cm_ref, coef_ref, st_ref, emit, skip_ref=None,
               issue_ahead=False):
    q = SSD_CHUNK
    a2 = coef_ref[:, 0:LANES]
    src2_t = coef_ref[:, LANES:2 * LANES]
    w_end = coef_ref[:, 2 * LANES:3 * LANES]
    dec = coef_ref[0:1, 3 * LANES:4 * LANES]
    row = lax.broadcasted_iota(jnp.int32, (q, q), 0)
    col = lax.broadcasted_iota(jnp.int32, (q, q), 1)
    seen = (row >= col) if direction == 0 else (col >= row)
    left = col < HEAD_DIM

    def group_products(g):
        gsl = slice(g * SSD_STATE, (g + 1) * SSD_STATE)
        cmg = cm_ref[:, gsl]
        bmg = bm_ref[:, gsl]
        cb = lax.dot_general(cmg, bmg, (((1,), (1,)), ((), ())), preferred_element_type=F32)
        h_in = st_ref[g]
        y_off = jnp.dot(cmg, h_in.astype(BF16), preferred_element_type=F32)
        return bmg, cb, h_in, y_off

    ahead = group_products(0) if issue_ahead else None
    for g in range(SSD_GROUPS):
        bmg, cb, h_in, y_off = ahead if issue_ahead else group_products(g)
        if issue_ahead and g + 1 < SSD_GROUPS:
            ahead = group_products(g + 1)
        ys, xws, decs = [], [], []
        for pr in range(SSD_HEADS_PER_GROUP // 2):
            pair = g * (SSD_HEADS_PER_GROUP // 2) + pr
            c0 = direction * SSD_HEADS + 2 * pair
            psl = slice(pair * LANES, (pair + 1) * LANES)
            xp = xs_ref[:, psl]
            ws, eas = [], []
            for c in (c0, c0 + 1):
                a_col = jnp.broadcast_to(a2[:, c:c + 1], (q, q))
                lmat = jnp.exp2(jnp.where(seen, a_col - src2_t[c:c + 1, :], -jnp.inf))
                ws.append((cb * lmat).astype(BF16))
                eas.append(jnp.exp2(a_col))
            zero = jnp.zeros_like(xp)
            x_diag = jnp.concatenate([jnp.where(left, xp, zero), jnp.where(left, zero, xp)], axis=0)
            y_diag = jnp.dot(jnp.concatenate(ws, axis=1), x_diag, preferred_element_type=F32)
            y = y_diag + y_off[:, pr * LANES:(pr + 1) * LANES] * jnp.where(left, eas[0], eas[1])
            xf = xp.astype(F32)
            if skip_ref is not None:
                y = y + skip_ref[:, psl] * xf
            ys.append(y)
            idx = jnp.where(left, c0, c0 + 1)
            xws.append((xf * jnp.take_along_axis(w_end, idx, axis=1)).astype(BF16))
            decs.append(jnp.where(left[0:1, :], jnp.broadcast_to(dec[:, c0:c0 + 1], (1, LANES)),
                                  jnp.broadcast_to(dec[:, c0 + 1:c0 + 2], (1, LANES))))
        emit(g, jnp.concatenate(ys, axis=1))
        new = lax.dot_general(bmg, jnp.concatenate(xws, axis=1), (((0,), (0,)), ((), ())),
                              preferred_element_type=F32)
        st_ref[g] = h_in * jnp.concatenate(decs, axis=1) + new


def _ssd_fwd_kernel(xs_ref, bm_ref, cm_ref, coef_ref, dsk_ref, y_ref, st_ref):
    @pl.when(pl.program_id(1) == 0)
    def _():
        st_ref[...] = jnp.zeros_like(st_ref)

    def emit(g, y):
        y_ref[:, g * SSD_GROUP_WIDTH:(g + 1) * SSD_GROUP_WIDTH] = y

    _ssd_chunk(0, xs_ref, bm_ref, cm_ref, coef_ref, st_ref, emit, skip_ref=dsk_ref,
               issue_ahead=True)


def _ssd_bwd_kernel(final, xs_ref, bm_ref, cm_ref, coef_ref, yf_ref, z_ref, h_ref, mod_ref,
                    nw_ref, wo_ref, fw_ref, o_ref, st_ref, u_ref):
    step = pl.program_id(1)

    @pl.when(step == 0)
    def _():
        st_ref[...] = jnp.zeros_like(st_ref)
        u_ref[...] = jnp.zeros_like(u_ref)

    y = jnp.dot(u_ref[(step + 1) % 2], wo_ref[...], preferred_element_type=F32)
    h = h_ref[...] + mod_ref[2:3, :] * y
    o_ref[...] = _rms_normalize(h) * fw_ref[...] if final else h

    def emit(g, y):
        sl = slice(g * SSD_GROUP_WIDTH, (g + 1) * SSD_GROUP_WIDTH)
        u = (y + yf_ref[:, sl]) * _silu(z_ref[:, sl].astype(F32))
        u_ref[step % 2, :, sl] = (_rms_normalize(u) * nw_ref[:, sl]).astype(BF16)

    _ssd_chunk(1, xs_ref, bm_ref, cm_ref, coef_ref, st_ref, emit)


def _ssd_scans(hs, mods, layer, z, xs, bm, cm, coef, d_skip, norm_w, w_out, final_w):
    bsz, t, d = hs.shape
    width = xs.shape[-1]
    gs_width = bm.shape[-1]
    q = SSD_CHUNK
    n_chunks = t // q
    ctx_chunks = CTX_LEN // q
    fwd = lambda b, i: (b, i, 0)
    bwd_chunk = lambda i: jnp.where(i < ctx_chunks, ctx_chunks - 1 - i, n_chunks - 1 + ctx_chunks - i)
    const = lambda a: pl.BlockSpec(a.shape, lambda b, i: (0,) * a.ndim)
    state = pltpu.VMEM((SSD_GROUPS, SSD_STATE, SSD_GROUP_WIDTH), F32)
    chunk_specs = lambda imap: [
        pl.BlockSpec((None, q, width), imap), pl.BlockSpec((None, q, gs_width), imap),
        pl.BlockSpec((None, q, gs_width), imap), pl.BlockSpec((None, q, coef.shape[-1]), imap)]

    y_f = pl.pallas_call(
        _ssd_fwd_kernel,
        grid=(bsz, n_chunks),
        in_specs=[*chunk_specs(fwd), const(d_skip)],
        out_specs=pl.BlockSpec((None, q, width), fwd),
        out_shape=jax.ShapeDtypeStruct((bsz, t, width), F32),
        scratch_shapes=[state],
        compiler_params=pltpu.CompilerParams(
            dimension_semantics=("parallel", "arbitrary"), vmem_limit_bytes=VMEM_LIMIT),
        name="ssd_fwd",
    )(xs, bm, cm, coef, d_skip)

    scanned = lambda b, i: (b, bwd_chunk(jnp.minimum(i, n_chunks - 1)), 0)
    behind = lambda i: bwd_chunk(jnp.maximum(i - 1, 0))
    final = final_w is not None
    if final:
        out_rows = t - CTX_LEN
        out_map = lambda b, i: (b, bwd_chunk(jnp.maximum(i - 1, ctx_chunks)) - ctx_chunks, 0)
    else:
        final_w = jnp.zeros((1, d), F32)
        out_rows = t
        out_map = lambda b, i: (b, behind(i), 0)
    return pl.pallas_call(
        functools.partial(_ssd_bwd_kernel, final),
        grid=(bsz, n_chunks + 1),
        in_specs=[
            *chunk_specs(scanned),
            pl.BlockSpec((None, q, width), scanned),
            pl.BlockSpec((None, q, width), scanned),
            pl.BlockSpec((None, q, d), lambda b, i: (b, behind(i), 0)),
            pl.BlockSpec((None, None, 3, d),
                         lambda b, i: (layer, _mod_row(behind(i) < ctx_chunks, b), 0, 0)),
            const(norm_w), const(w_out), const(final_w),
        ],
        out_specs=pl.BlockSpec((None, q, d), out_map),
        out_shape=jax.ShapeDtypeStruct((bsz, out_rows, d), F32),
        scratch_shapes=[state, pltpu.VMEM((2, q, width), BF16)],
        compiler_params=pltpu.CompilerParams(
            dimension_semantics=("parallel", "arbitrary"), vmem_limit_bytes=VMEM_LIMIT),
        name="ssd_bwd",
    )(xs, bm, cm, coef, y_f, z, hs, mods, norm_w, w_out, final_w)


def _rope_tables(seq_len):
    quarter = HEAD_DIM // 4
    pos = jnp.arange(seq_len, dtype=jnp.int32)
    row = (pos // GRID_W).astype(F32)
    col = (pos % GRID_W).astype(F32)
    inv_freq = ROPE_BASE ** (-jnp.arange(0, 2 * quarter, 2, dtype=F32) / (2 * quarter))
    lane = jnp.arange(LANES)
    in_head = lane % HEAD_DIM
    p = jnp.where((in_head < HEAD_DIM // 2)[None, :], row[:, None], col[:, None])
    ang = p * inv_freq[lane % quarter][None, :]
    first = ((lane % (2 * quarter)) < quarter)[None, :]
    cos = jnp.cos(ang)
    sin = jnp.sin(ang)
    sa = jnp.where(first, -sin, 0.0)
    sb = jnp.where(first, 0.0, sin)
    ident = jnp.ones((CTX_LEN, LANES), F32)
    zeros = jnp.zeros((CTX_LEN, LANES), F32)
    return (jnp.concatenate([ident, cos], 0), jnp.concatenate([zeros, sa], 0),
            jnp.concatenate([zeros, sb], 0))


def kernel(x, c, ctx, c_ctx, w_ada, b_ada, attn_w_in, attn_sink, attn_w_out, ssd_w_in, ssd_conv_w,
           ssd_conv_b, ssd_dt_bias, ssd_a_log, ssd_d, ssd_norm_w, ssd_w_out, final_norm_w):
    bsz, seq_len, d = x.shape
    depth = w_ada.shape[0]
    assert ctx.shape[1] == CTX_LEN and CTX_LEN % ROW_TILE == 0 and seq_len % ROW_TILE == 0
    assert bsz <= 8 and depth % 2 == 0

    cond = jnp.zeros((16, d), F32).at[:bsz].set(c).at[8].set(c_ctx)
    mods = _modulation_tables(cond, w_ada, b_ada).reshape(depth, 16, 3, d)
    cos, sa, sb = _rope_tables(seq_len)
    hs = (ctx, x)

    pad_lanes = lambda a: jnp.pad(a, ((0, 0), (0, LANES - a.shape[-1])))

    for i in range(depth):
        j = i // 2
        if i % 2 == 0:
            q, k, v, g = _attn_in(hs, mods, i, attn_w_in[j].astype(BF16), cos, sa, sb)
            hs = _attn_core(hs, mods, i, q, k, v, g, attn_sink[j], attn_w_out[j].astype(BF16))
        else:
            w_in = jnp.pad(ssd_w_in[j], ((0, 0), (0, LANES - 2 * SSD_HEADS))).astype(BF16)
            z, xs, bm, cm, coef = _ssd_in(
                hs, mods, i, w_in, ssd_conv_w[j], ssd_conv_b[j][None, :],
                pad_lanes(ssd_dt_bias[j].reshape(1, -1)), pad_lanes(ssd_a_log[j].reshape(1, -1)))
            hs = _ssd_scans(
                hs, mods, i, z, xs, bm, cm, coef,
                jnp.repeat(ssd_d[j], HEAD_DIM)[None, :], ssd_norm_w[j][None, :],
                ssd_w_out[j].astype(BF16), final_norm_w[None, :] if i == depth - 1 else None)
    return hs
```

```python
import functools
import math

import jax
import jax.numpy as jnp
from jax import lax
from jax.experimental import pallas as pl
from jax.experimental.pallas import tpu as pltpu

F32 = jnp.float32
BF16 = jnp.bfloat16

EPS = 1e-6
CTX_LEN = 256
GRID_W = 64
ROPE_BASE = 10000.0

HEAD_DIM = 64
ATTN_HEADS = 16
ATTN_KV_HEADS = 4
ATTN_GROUP = ATTN_HEADS // ATTN_KV_HEADS
ATTN_BLOCK = 128
LANES = 128
SUBLANES = 8

SSD_HEADS = 32
SSD_GROUPS = 8
SSD_HEADS_PER_GROUP = SSD_HEADS // SSD_GROUPS
SSD_STATE = 128
SSD_CONV = 5
SSD_CHUNK = 128
SSD_GROUP_WIDTH = SSD_HEADS_PER_GROUP * HEAD_DIM

ROW_TILE = 256
CONV_COL_TILE = 512
MASKED = -1e30
VMEM_LIMIT = 56 * 1024 * 1024


def _silu(x):
    return x / (1.0 + jnp.exp2(x * -math.log2(math.e)))


def _rms_normalize(x):
    ms = jnp.mean(x * x, axis=-1, keepdims=True)
    return x * lax.rsqrt(ms + EPS)


def _norm_modulate(x, mod_ref):
    return _rms_normalize(x) * (1.0 + mod_ref[1:2, :]) + mod_ref[0:1, :]


def _mod_row(is_ctx, b):
    return jnp.where(is_ctx, 8, b)


def _mod_kernel(cond_ref, w_ref, b_ref, o_ref):
    cnd = cond_ref[...]
    o_ref[...] = jnp.dot(_silu(cnd), w_ref[...], preferred_element_type=F32,
                         precision=lax.Precision.HIGHEST) + b_ref[...]


def _modulation_tables(cond, w_ada, b_ada):
    depth, d, d3 = w_ada.shape
    return pl.pallas_call(
        _mod_kernel,
        grid=(depth, d3 // d),
        in_specs=[
            pl.BlockSpec((16, d), lambda l, j: (0, 0)),
            pl.BlockSpec((None, d, d), lambda l, j: (l, 0, j)),
            pl.BlockSpec((None, 1, d), lambda l, j: (l, 0, j)),
        ],
        out_specs=pl.BlockSpec((None, 16, d), lambda l, j: (l, 0, j)),
        out_shape=jax.ShapeDtypeStruct((depth, 16, d3), F32),
        compiler_params=pltpu.CompilerParams(vmem_limit_bytes=VMEM_LIMIT),
        name="adaln_tables",
    )(cond, w_ada, b_ada.reshape(depth, 1, d3))


def _attn_in_kernel(split, *refs):
    if split:
        c_ref, h_ref, *refs = refs
        h = jnp.where(pl.program_id(1) < CTX_LEN // ROW_TILE, c_ref[...], h_ref[...])
    else:
        h_ref, *refs = refs
        h = h_ref[...]
    mod_ref, w_ref, cos_ref, sa_ref, sb_ref, q_ref, k_ref, v_ref, g_ref = refs
    d = h_ref.shape[-1]
    kvw = ATTN_KV_HEADS * HEAD_DIM
    ub = _norm_modulate(h, mod_ref).astype(BF16)
    cos = cos_ref[...]
    sa = sa_ref[...]
    sb = sb_ref[...]

    def rope(t):
        return t * cos + pltpu.roll(t, LANES - 16, 1) * sa + pltpu.roll(t, 16, 1) * sb

    left = lax.broadcasted_iota(jnp.int32, (h_ref.shape[0], LANES), 1) < HEAD_DIM

    def store_doubled(ref, j, t):
        sw = pltpu.roll(t, HEAD_DIM, 1)
        ref[:, (2 * j) * LANES:(2 * j + 1) * LANES] = jnp.where(left, t, sw).astype(BF16)
        ref[:, (2 * j + 1) * LANES:(2 * j + 2) * LANES] = jnp.where(left, sw, t).astype(BF16)

    q = jnp.dot(ub, w_ref[:, 0:d], preferred_element_type=F32)
    scale = HEAD_DIM ** -0.5 * math.log2(math.e)
    for j in range(d // LANES):
        sl = slice(j * LANES, (j + 1) * LANES)
        q_ref[:, sl] = (rope(q[:, sl]) * scale).astype(BF16)
    k = jnp.dot(ub, w_ref[:, d:d + kvw], preferred_element_type=F32)
    v = jnp.dot(ub, w_ref[:, d + kvw:d + 2 * kvw], preferred_element_type=F32)
    for j in range(kvw // LANES):
        sl = slice(j * LANES, (j + 1) * LANES)
        store_doubled(k_ref, j, rope(k[:, sl]))
        store_doubled(v_ref, j, v[:, sl])
    g = jnp.dot(ub, w_ref[:, d + 2 * kvw:], preferred_element_type=F32)
    g_ref[...] = _silu(g).astype(BF16)


def _stream_specs(hs, rows):
    if not isinstance(hs, tuple):
        return [pl.BlockSpec((None, rows, hs.shape[-1]), lambda b, i: (b, i, 0))], [hs], hs.shape[1]
    ctx, x = hs
    n_ctx = CTX_LEN // rows
    d = x.shape[-1]
    return ([pl.BlockSpec((None, rows, d), lambda b, i: (b, jnp.minimum(i, n_ctx - 1), 0)),
             pl.BlockSpec((None, rows, d), lambda b, i: (b, jnp.maximum(i - n_ctx, 0), 0))],
            [ctx, x], CTX_LEN + x.shape[1])


def _attn_in(hs, mods, layer, w_in, cos, sa, sb):
    stream_specs, stream, t = _stream_specs(hs, ROW_TILE)
    bsz, _, d = stream[0].shape
    kvw = ATTN_KV_HEADS * HEAD_DIM
    n_in = w_in.shape[1]
    ctx_tiles = CTX_LEN // ROW_TILE
    row_spec = lambda width: pl.BlockSpec((None, ROW_TILE, width), lambda b, i: (b, i, 0))
    tab_spec = pl.BlockSpec((ROW_TILE, LANES), lambda b, i: (i, 0))
    return pl.pallas_call(
        functools.partial(_attn_in_kernel, len(stream) == 2),
        grid=(bsz, t // ROW_TILE),
        in_specs=[
            *stream_specs,
            pl.BlockSpec((None, None, 3, d), lambda b, i: (layer, _mod_row(i < ctx_tiles, b), 0, 0)),
            pl.BlockSpec((d, n_in), lambda b, i: (0, 0)),
            tab_spec, tab_spec, tab_spec,
        ],
        out_specs=[row_spec(d), row_spec(2 * kvw), row_spec(2 * kvw), row_spec(d)],
        out_shape=[
            jax.ShapeDtypeStruct((bsz, t, d), BF16),
            jax.ShapeDtypeStruct((bsz, t, 2 * kvw), BF16),
            jax.ShapeDtypeStruct((bsz, t, 2 * kvw), BF16),
            jax.ShapeDtypeStruct((bsz, t, d), BF16),
        ],
        compiler_params=pltpu.CompilerParams(
            dimension_semantics=("parallel", "parallel"), vmem_limit_bytes=VMEM_LIMIT),
        name="attn_in",
    )(*stream, mods, w_in, cos, sa, sb)


def _attn_core_kernel(n_lat_blocks, split, sink_ref, q_ref, kp_ref, kc_ref, kn_ref, kx_ref,
                      vp_ref, vc_ref, vn_ref, vx_ref, g_ref, *refs):
    if split:
        c_ref, h_ref, mod_ref, wo_ref, o_ref, kcat, vdiag = refs
    else:
        h_ref, mod_ref, wo_ref, o_ref, kcat, vdiag = refs
        c_ref = h_ref
    blk = pl.program_id(1)
    ctx_blocks = CTX_LEN // ATTN_BLOCK
    j = blk - ctx_blocks
    nb = ATTN_BLOCK
    pairs = ATTN_GROUP // 2
    log2e = math.log2(math.e)
    nt_dims = (((1,), (1,)), ((), ()))

    def left(rows):
        return lax.broadcasted_iota(jnp.int32, (rows, LANES), 1) < HEAD_DIM

    def split_heads(t):
        zero = jnp.zeros_like(t)
        return jnp.where(left(t.shape[0]), t, zero), jnp.where(left(t.shape[0]), zero, t)

    def stacked_queries(kh):
        rows = []
        for pr in range(pairs):
            rows.extend(split_heads(q_ref[:, (kh * pairs + pr) * LANES:(kh * pairs + pr + 1) * LANES]))
        return jnp.concatenate(rows, axis=0)

    def softmax(kh, s):
        sink = jnp.concatenate(
            [jnp.full((nb, 1), sink_ref[kh * ATTN_GROUP + r] * log2e, F32) for r in range(ATTN_GROUP)],
            axis=0)
        m = jnp.maximum(jnp.max(s, axis=-1, keepdims=True), sink)
        p = jnp.exp2(s - m)
        den = jnp.sum(p, axis=-1, keepdims=True) + jnp.exp2(sink - m)
        return p.astype(BF16), 1.0 / den

    def weighted_values(kh, pb, r, n_keys):
        lhs = jnp.concatenate(
            [jnp.concatenate([pb[2 * pr * nb:(2 * pr + 1) * nb], pb[(2 * pr + 1) * nb:(2 * pr + 2) * nb]],
                             axis=1) for pr in range(pairs)], axis=0)
        o = jnp.dot(lhs, vdiag[kh, 0:2 * n_keys, :], preferred_element_type=F32)
        gated = []
        for pr in range(pairs):
            lo, mid, hi = 2 * pr * nb, (2 * pr + 1) * nb, (2 * pr + 2) * nb
            rr = jnp.where(left(nb), r[lo:mid], r[mid:hi])
            sl = slice((kh * pairs + pr) * LANES, (kh * pairs + pr + 1) * LANES)
            gated.append((o[pr * nb:(pr + 1) * nb] * rr * g_ref[:, sl].astype(F32)).astype(BF16))
        rows = slice(kh * pairs * LANES, (kh + 1) * pairs * LANES)
        return jnp.dot(jnp.concatenate(gated, axis=1), wo_ref[rows, :], preferred_element_type=F32)

    def attend(key_blocks, bias, res_ref):
        n_keys = sum(k_ref.shape[0] for k_ref, _ in key_blocks)
        r0 = 0
        for k_ref, v_ref in key_blocks:
            r1 = r0 + k_ref.shape[0]
            kcat[r0:r1, :] = k_ref[...]
            for kh in range(ATTN_KV_HEADS):
                top, bot = split_heads(v_ref[:, kh * LANES:(kh + 1) * LANES])
                vdiag[kh, r0:r1, :] = top
                vdiag[kh, n_keys + r0:n_keys + r1, :] = bot
            r0 = r1

        def logits(kh):
            s = lax.dot_general(stacked_queries(kh), kcat[0:n_keys, kh * LANES:(kh + 1) * LANES],
                                nt_dims, preferred_element_type=F32)
            if not bias:
                return s
            return jnp.concatenate(
                [s[:, i * nb:(i + 1) * nb] + bias[i] if i in bias else s[:, i * nb:(i + 1) * nb]
                 for i in range(n_keys // nb)], axis=1)

        s_next = logits(0)
        pending = None
        y = None
        for kh in range(ATTN_KV_HEADS):
            s = s_next
            if kh + 1 < ATTN_KV_HEADS:
                s_next = logits(kh + 1)
            pb, r = softmax(kh, s)
            if pending is not None:
                part = weighted_values(*pending, n_keys)
                y = part if y is None else y + part
            pending = (kh, pb, r)
        y = y + weighted_values(*pending, n_keys)
        o_ref[...] = res_ref[...] + mod_ref[2:3, :] * y

    @pl.when(blk >= ctx_blocks)
    def _():
        row = lax.broadcasted_iota(jnp.int32, (nb, nb), 0)
        col = lax.broadcasted_iota(jnp.int32, (nb, nb), 1)
        bias_prev = jnp.where((col >= row) & (j > 0), 0.0, MASKED).astype(F32)
        bias_next = jnp.where((col <= row) & (j < n_lat_blocks - 1), 0.0, MASKED).astype(F32)
        attend(((kp_ref, vp_ref), (kc_ref, vc_ref), (kn_ref, vn_ref), (kx_ref, vx_ref)),
               {0: jnp.concatenate([bias_prev] * ATTN_GROUP, axis=0),
                2: jnp.concatenate([bias_next] * ATTN_GROUP, axis=0)}, h_ref)

    @pl.when(blk < ctx_blocks)
    def _():
        attend(((kx_ref, vx_ref),), {}, c_ref)


def _attn_core(hs, mods, layer, q, k, v, g, sink, w_out):
    nb = ATTN_BLOCK
    stream_specs, stream, t = _stream_specs(hs, nb)
    bsz, _, d = stream[0].shape
    kvw = ATTN_KV_HEADS * LANES
    ctx_blocks = CTX_LEN // nb
    n_blocks = t // nb
    n_lat = n_blocks - ctx_blocks
    blk_spec = lambda width, imap: pl.BlockSpec((None, nb, width), imap)
    own = lambda b, i: (b, i, 0)
    prev = lambda b, i: (b, jnp.maximum(i - 1, ctx_blocks), 0)
    nxt = lambda b, i: (b, jnp.minimum(jnp.maximum(i + 1, ctx_blocks), n_blocks - 1), 0)
    ctx_spec = pl.BlockSpec((None, CTX_LEN, kvw), lambda b, i: (b, 0, 0))
    kv_specs = [blk_spec(kvw, prev), blk_spec(kvw, own), blk_spec(kvw, nxt), ctx_spec]
    return pl.pallas_call(
        functools.partial(_attn_core_kernel, n_lat, len(stream) == 2),
        grid=(bsz, n_blocks),
        in_specs=[
            pl.BlockSpec(memory_space=pltpu.SMEM),
            blk_spec(d, own),
            *kv_specs, *kv_specs,
            blk_spec(d, own),
            *stream_specs,
            pl.BlockSpec((None, None, 3, d), lambda b, i: (layer, _mod_row(i < ctx_blocks, b), 0, 0)),
            pl.BlockSpec(w_out.shape, lambda b, i: (0, 0)),
        ],
        out_specs=blk_spec(d, own),
        out_shape=jax.ShapeDtypeStruct((bsz, t, d), F32),
        scratch_shapes=[
            pltpu.VMEM((3 * nb + CTX_LEN, kvw), BF16),
            pltpu.VMEM((ATTN_KV_HEADS, 2 * (3 * nb + CTX_LEN), LANES), BF16),
        ],
        compiler_params=pltpu.CompilerParams(
            dimension_semantics=("parallel", "parallel"), vmem_limit_bytes=VMEM_LIMIT),
        name="attn_core",
    )(sink, q, k, k, k, k, v, v, v, v, g, *stream, mods, w_out)


def _ssd_in_kernel(h_ref, hp_ref, hn_ref, mod_ref, w_ref, cw_ref, cb_ref, dtb_ref, alog_ref,
                   z_ref, xs_ref, bm_ref, cm_ref, coef_ref):
    blk = pl.program_id(1)
    n_tiles = pl.num_programs(1)
    ctx_tiles = CTX_LEN // ROW_TILE
    halo = SUBLANES
    tm = ROW_TILE
    width = xs_ref.shape[-1]
    gs_width = bm_ref.shape[-1]

    prev_ok = (blk > ctx_tiles) | ((blk > 0) & (blk < ctx_tiles))
    next_ok = ((blk >= ctx_tiles) & (blk < n_tiles - 1)) | (blk < ctx_tiles - 1)
    u_main = _norm_modulate(h_ref[...], mod_ref)
    u_all = jnp.concatenate(
        [_norm_modulate(hp_ref[...], mod_ref) * prev_ok.astype(F32), u_main,
         _norm_modulate(hn_ref[...], mod_ref) * next_ok.astype(F32)], axis=0).astype(BF16)
    ub = u_main.astype(BF16)

    conv_dim = cw_ref.shape[-1]
    dt_raw = jnp.dot(ub, w_ref[:, width + conv_dim:], preferred_element_type=F32) + dtb_ref[...]
    dt = jnp.maximum(dt_raw, 0.0) + jnp.log1p(jnp.exp(-jnp.abs(dt_raw)))
    neg_a = -jnp.exp(alog_ref[...])
    for k in range(tm // SSD_CHUNK):
        rows = slice(k * SSD_CHUNK, (k + 1) * SSD_CHUNK)
        coef_ref[rows, :] = _scan_coefficients(dt[rows], neg_a)
    z_tile = width // (conv_dim // CONV_COL_TILE)

    pad = SSD_CONV // 2
    n_rows = tm + 2 * halo
    sub = lax.broadcasted_iota(jnp.int32, (1, SUBLANES, CONV_COL_TILE), 1)
    for c in range(conv_dim // CONV_COL_TILE):
        cs = slice(c * CONV_COL_TILE, (c + 1) * CONV_COL_TILE)
        pre = jnp.dot(u_all, w_ref[:, width + c * CONV_COL_TILE:width + (c + 1) * CONV_COL_TILE],
                      preferred_element_type=F32)
        tiles = pre.reshape(n_rows // SUBLANES, SUBLANES, CONV_COL_TILE)
        n_out = tm // SUBLANES
        own = tiles[1:1 + n_out]
        acc = cb_ref[:, cs] + cw_ref[pad:pad + 1, cs] * own
        for kk in range(SSD_CONV):
            off = kk - pad
            if off == 0:
                continue
            from_own = (sub >= off) if off > 0 else (sub < SUBLANES + off)
            nbr = 1 + (1 if off > 0 else -1)
            mixed = jnp.where(from_own, own, tiles[nbr:nbr + n_out])
            acc = acc + cw_ref[kk:kk + 1, cs] * pltpu.roll(mixed, (-off) % SUBLANES, 1)
        out = _silu(acc).reshape(tm, CONV_COL_TILE).astype(BF16)
        zs = slice(c * z_tile, (c + 1) * z_tile)
        z_ref[:, zs] = jnp.dot(ub, w_ref[:, zs], preferred_element_type=F32).astype(BF16)
        lo = c * CONV_COL_TILE
        if lo < width:
            xs_ref[:, lo:lo + CONV_COL_TILE] = out
        elif lo < width + gs_width:
            bm_ref[:, lo - width:lo - width + CONV_COL_TILE] = out
        else:
            cm_ref[:, lo - width - gs_width:lo - width - gs_width + CONV_COL_TILE] = out


def _ssd_in(hs, mods, layer, w_in, conv_w, conv_b, dt_bias, a_log):
    bsz, t, d = hs.shape
    width = w_in.shape[1] - conv_w.shape[1] - LANES
    gs_width = SSD_GROUPS * SSD_STATE
    n_tiles = t // ROW_TILE
    ctx_tiles = CTX_LEN // ROW_TILE
    per_tile = ROW_TILE // SUBLANES
    n_halo_blocks = t // SUBLANES
    row_spec = lambda w_: pl.BlockSpec((None, ROW_TILE, w_), lambda b, i: (b, i, 0))
    full = lambda a: pl.BlockSpec(a.shape, lambda b, i: (0,) * a.ndim)
    return pl.pallas_call(
        _ssd_in_kernel,
        grid=(bsz, n_tiles),
        in_specs=[
            row_spec(d),
            pl.BlockSpec((None, SUBLANES, d), lambda b, i: (b, jnp.maximum(i * per_tile - 1, 0), 0)),
            pl.BlockSpec((None, SUBLANES, d),
                         lambda b, i: (b, jnp.minimum((i + 1) * per_tile, n_halo_blocks - 1), 0)),
            pl.BlockSpec((None, None, 3, d), lambda b, i: (layer, _mod_row(i < ctx_tiles, b), 0, 0)),
            full(w_in), full(conv_w), full(conv_b), full(dt_bias), full(a_log),
        ],
        out_specs=[row_spec(width), row_spec(width), row_spec(gs_width), row_spec(gs_width),
                   row_spec(4 * LANES)],
        out_shape=[
            jax.ShapeDtypeStruct((bsz, t, width), BF16),
            jax.ShapeDtypeStruct((bsz, t, width), BF16),
            jax.ShapeDtypeStruct((bsz, t, gs_width), BF16),
            jax.ShapeDtypeStruct((bsz, t, gs_width), BF16),
            jax.ShapeDtypeStruct((bsz, t, 4 * LANES), F32),
        ],
        compiler_params=pltpu.CompilerParams(
            dimension_semantics=("parallel", "parallel"), vmem_limit_bytes=VMEM_LIMIT),
        name="ssd_in",
    )(hs, hs, hs, mods, w_in, conv_w, conv_b, dt_bias, a_log)


def _scan_coefficients(dt, neg_a):
    q = SSD_CHUNK
    log2e = math.log2(math.e)
    row = lax.broadcasted_iota(jnp.int32, (q, q), 0)
    col = lax.broadcasted_iota(jnp.int32, (q, q), 1)
    both = jnp.concatenate([row >= col, col >= row], axis=0).astype(F32)
    cum = jnp.dot(both, dt * neg_a, preferred_element_type=F32, precision=lax.Precision.HIGHEST)
    fwd_lane = lax.broadcasted_iota(jnp.int32, (q, LANES), 1) < SSD_HEADS
    a = jnp.where(fwd_lane, cum[0:q], cum[q:2 * q])
    a_tot = jnp.where(fwd_lane[0:1], cum[q - 1:q], cum[q:q + 1])
    return jnp.concatenate(
        [a * log2e, ((a - jnp.log(dt)) * log2e).T, jnp.exp(a_tot - a) * dt,
         jnp.broadcast_to(jnp.exp(a_tot), (q, LANES))], axis=1)


def _ssd_chunk(direction, xs_ref, bm_ref, cm_ref, coef_ref, st_ref, emit, skip_ref=None,
               issue_ahead=False):
    q = SSD_CHUNK
    a2 = coef_ref[:, 0:LANES]
    src2_t = coef_ref[:, LANES:2 * LANES]
    w_end = coef_ref[:, 2 * LANES:3 * LANES]
    dec = coef_ref[0:1, 3 * LANES:4 * LANES]
    row = lax.broadcasted_iota(jnp.int32, (q, q), 0)
    col = lax.broadcasted_iota(jnp.int32, (q, q), 1)
    seen = (row >= col) if direction == 0 else (col >= row)
    left = col < HEAD_DIM

    def group_products(g):
        gsl = slice(g * SSD_STATE, (g + 1) * SSD_STATE)
        cmg = cm_ref[:, gsl]
        bmg = bm_ref[:, gsl]
        cb = lax.dot_general(cmg, bmg, (((1,), (1,)), ((), ())), preferred_element_type=F32)
        h_in = st_ref[g]
        y_off = jnp.dot(cmg, h_in.astype(BF16), preferred_element_type=F32)
        return bmg, cb, h_in, y_off

    ahead = group_products(0) if issue_ahead else None
    for g in range(SSD_GROUPS):
        bmg, cb, h_in, y_off = ahead if issue_ahead else group_products(g)
        if issue_ahead and g + 1 < SSD_GROUPS:
            ahead = group_products(g + 1)
        ys, xws, decs = [], [], []
        for pr in range(SSD_HEADS_PER_GROUP // 2):
            pair = g * (SSD_HEADS_PER_GROUP // 2) + pr
            c0 = direction * SSD_HEADS + 2 * pair
            psl = slice(pair * LANES, (pair + 1) * LANES)
            xp = xs_ref[:, psl]
            ws, a_cols = [], []
            for c in (c0, c0 + 1):
                a_col = jnp.broadcast_to(a2[:, c:c + 1], (q, q))
                lmat = jnp.exp2(jnp.where(seen, a_col - src2_t[c:c + 1, :], -jnp.inf))
                ws.append((cb * lmat).astype(BF16))
                a_cols.append(a_col)
            zero = jnp.zeros_like(xp)
            x_diag = jnp.concatenate([jnp.where(left, xp, zero), jnp.where(left, zero, xp)], axis=0)
            y_diag = jnp.dot(jnp.concatenate(ws, axis=1), x_diag, preferred_element_type=F32)
            carried = jnp.exp2(jnp.where(left, a_cols[0], a_cols[1]))
            y = y_diag + y_off[:, pr * LANES:(pr + 1) * LANES] * carried
            xf = xp.astype(F32)
            if skip_ref is not None:
                y = y + skip_ref[:, psl] * xf
            ys.append(y)
            idx = jnp.where(left, c0, c0 + 1)
            xws.append((xf * jnp.take_along_axis(w_end, idx, axis=1)).astype(BF16))
            decs.append(jnp.where(left[0:1, :], jnp.broadcast_to(dec[:, c0:c0 + 1], (1, LANES)),
                                  jnp.broadcast_to(dec[:, c0 + 1:c0 + 2], (1, LANES))))
        emit(g, jnp.concatenate(ys, axis=1))
        new = lax.dot_general(bmg, jnp.concatenate(xws, axis=1), (((0,), (0,)), ((), ())),
                              preferred_element_type=F32)
        st_ref[g] = h_in * jnp.concatenate(decs, axis=1) + new


def _ssd_fwd_kernel(xs_ref, bm_ref, cm_ref, coef_ref, dsk_ref, y_ref, st_ref):
    @pl.when(pl.program_id(1) == 0)
    def _():
        st_ref[...] = jnp.zeros_like(st_ref)

    def emit(g, y):
        y_ref[:, g * SSD_GROUP_WIDTH:(g + 1) * SSD_GROUP_WIDTH] = y

    _ssd_chunk(0, xs_ref, bm_ref, cm_ref, coef_ref, st_ref, emit, skip_ref=dsk_ref,
               issue_ahead=True)


def _ssd_bwd_kernel(final, xs_ref, bm_ref, cm_ref, coef_ref, yf_ref, z_ref, h_ref, mod_ref,
                    nw_ref, wo_ref, fw_ref, o_ref, st_ref, u_ref):
    step = pl.program_id(1)

    @pl.when(step == 0)
    def _():
        st_ref[...] = jnp.zeros_like(st_ref)
        u_ref[...] = jnp.zeros_like(u_ref)

    y = jnp.dot(u_ref[(step + 1) % 2], wo_ref[...], preferred_element_type=F32)
    h = h_ref[...] + mod_ref[2:3, :] * y
    o_ref[...] = _rms_normalize(h) * fw_ref[...] if final else h

    def emit(g, y):
        sl = slice(g * SSD_GROUP_WIDTH, (g + 1) * SSD_GROUP_WIDTH)
        u = (y + yf_ref[:, sl]) * _silu(z_ref[:, sl].astype(F32))
        u_ref[step % 2, :, sl] = (_rms_normalize(u) * nw_ref[:, sl]).astype(BF16)

    _ssd_chunk(1, xs_ref, bm_ref, cm_ref, coef_ref, st_ref, emit)


def _ssd_scans(hs, mods, layer, z, xs, bm, cm, coef, d_skip, norm_w, w_out, final_w):
    bsz, t, d = hs.shape
    width = xs.shape[-1]
    gs_width = bm.shape[-1]
    q = SSD_CHUNK
    n_chunks = t // q
    ctx_chunks = CTX_LEN // q
    fwd = lambda b, i: (b, i, 0)
    bwd_chunk = lambda i: jnp.where(i < ctx_chunks, ctx_chunks - 1 - i, n_chunks - 1 + ctx_chunks - i)
    const = lambda a: pl.BlockSpec(a.shape, lambda b, i: (0,) * a.ndim)
    state = pltpu.VMEM((SSD_GROUPS, SSD_STATE, SSD_GROUP_WIDTH), F32)
    chunk_specs = lambda imap: [
        pl.BlockSpec((None, q, width), imap), pl.BlockSpec((None, q, gs_width), imap),
        pl.BlockSpec((None, q, gs_width), imap), pl.BlockSpec((None, q, coef.shape[-1]), imap)]

    y_f = pl.pallas_call(
        _ssd_fwd_kernel,
        grid=(bsz, n_chunks),
        in_specs=[*chunk_specs(fwd), const(d_skip)],
        out_specs=pl.BlockSpec((None, q, width), fwd),
        out_shape=jax.ShapeDtypeStruct((bsz, t, width), F32),
        scratch_shapes=[state],
        compiler_params=pltpu.CompilerParams(
            dimension_semantics=("parallel", "arbitrary"), vmem_limit_bytes=VMEM_LIMIT),
        name="ssd_fwd",
    )(xs, bm, cm, coef, d_skip)

    scanned = lambda b, i: (b, bwd_chunk(jnp.minimum(i, n_chunks - 1)), 0)
    behind = lambda i: bwd_chunk(jnp.maximum(i - 1, 0))
    final = final_w is not None
    if final:
        out_rows = t - CTX_LEN
        out_map = lambda b, i: (b, bwd_chunk(jnp.maximum(i - 1, ctx_chunks)) - ctx_chunks, 0)
    else:
        final_w = jnp.zeros((1, d), F32)
        out_rows = t
        out_map = lambda b, i: (b, behind(i), 0)
    return pl.pallas_call(
        functools.partial(_ssd_bwd_kernel, final),
        grid=(bsz, n_chunks + 1),
        in_specs=[
            *chunk_specs(scanned),
            pl.BlockSpec((None, q, width), scanned),
            pl.BlockSpec((None, q, width), scanned),
            pl.BlockSpec((None, q, d), lambda b, i: (b, behind(i), 0)),
            pl.BlockSpec((None, None, 3, d),
                         lambda b, i: (layer, _mod_row(behind(i) < ctx_chunks, b), 0, 0)),
            const(norm_w), const(w_out), const(final_w),
        ],
        out_specs=pl.BlockSpec((None, q, d), out_map),
        out_shape=jax.ShapeDtypeStruct((bsz, out_rows, d), F32),
        scratch_shapes=[state, pltpu.VMEM((2, q, width), BF16)],
        compiler_params=pltpu.CompilerParams(
            dimension_semantics=("parallel", "arbitrary"), vmem_limit_bytes=VMEM_LIMIT),
        name="ssd_bwd",
    )(xs, bm, cm, coef, y_f, z, hs, mods, norm_w, w_out, final_w)


def _rope_tables(seq_len):
    quarter = HEAD_DIM // 4
    pos = jnp.arange(seq_len, dtype=jnp.int32)
    row = (pos // GRID_W).astype(F32)
    col = (pos % GRID_W).astype(F32)
    inv_freq = ROPE_BASE ** (-jnp.arange(0, 2 * quarter, 2, dtype=F32) / (2 * quarter))
    lane = jnp.arange(LANES)
    in_head = lane % HEAD_DIM
    p = jnp.where((in_head < HEAD_DIM // 2)[None, :], row[:, None], col[:, None])
    ang = p * inv_freq[lane % quarter][None, :]
    first = ((lane % (2 * quarter)) < quarter)[None, :]
    cos = jnp.cos(ang)
    sin = jnp.sin(ang)
    sa = jnp.where(first, -sin, 0.0)
    sb = jnp.where(first, 0.0, sin)
    ident = jnp.ones((CTX_LEN, LANES), F32)
    zeros = jnp.zeros((CTX_LEN, LANES), F32)
    return (jnp.concatenate([ident, cos], 0), jnp.concatenate([zeros, sa], 0),
            jnp.concatenate([zeros, sb], 0))


def kernel(x, c, ctx, c_ctx, w_ada, b_ada, attn_w_in, attn_sink, attn_w_out, ssd_w_in, ssd_conv_w,
           ssd_conv_b, ssd_dt_bias, ssd_a_log, ssd_d, ssd_norm_w, ssd_w_out, final_norm_w):
    bsz, seq_len, d = x.shape
    depth = w_ada.shape[0]
    assert ctx.shape[1] == CTX_LEN and CTX_LEN % ROW_TILE == 0 and seq_len % ROW_TILE == 0
    assert bsz <= 8 and depth % 2 == 0

    cond = jnp.zeros((16, d), F32).at[:bsz].set(c).at[8].set(c_ctx)
    mods = _modulation_tables(cond, w_ada, b_ada).reshape(depth, 16, 3, d)
    cos, sa, sb = _rope_tables(seq_len)
    hs = (ctx, x)

    pad_lanes = lambda a: jnp.pad(a, ((0, 0), (0, LANES - a.shape[-1])))

    for i in range(depth):
        j = i // 2
        if i % 2 == 0:
            q, k, v, g = _attn_in(hs, mods, i, attn_w_in[j].astype(BF16), cos, sa, sb)
            hs = _attn_core(hs, mods, i, q, k, v, g, attn_sink[j], attn_w_out[j].astype(BF16))
        else:
            w_in = jnp.pad(ssd_w_in[j], ((0, 0), (0, LANES - 2 * SSD_HEADS))).astype(BF16)
            z, xs, bm, cm, coef = _ssd_in(
                hs, mods, i, w_in, ssd_conv_w[j], ssd_conv_b[j][None, :],
                pad_lanes(ssd_dt_bias[j].reshape(1, -1)), pad_lanes(ssd_a_log[j].reshape(1, -1)))
            hs = _ssd_scans(
                hs, mods, i, z, xs, bm, cm, coef,
                jnp.repeat(ssd_d[j], HEAD_DIM)[None, :], ssd_norm_w[j][None, :],
                ssd_w_out[j].astype(BF16), final_norm_w[None, :] if i == depth - 1 else None)
    return hs
```
